```python
import math
import jax, jax.numpy as jnp
from jax import lax
import numpy as np

D_MODEL = 2048
BATCH = 4
SEQ = 4096
DEPTH = 4

N_A_LAYERS = DEPTH // 2
N_B_LAYERS = DEPTH - N_A_LAYERS
MIX_WIDTH = D_MODEL
N_MEM = 256
MEM_HEADS = 4
MEM_HEAD_DIM = 128
MEM_WIDTH = MEM_HEADS * MEM_HEAD_DIM
TOK_WIDTH = MIX_WIDTH - MEM_WIDTH
S5_GROUP = 16
S5_GROUPS = TOK_WIDTH // S5_GROUP
S5_STATE = 64
S5_DT_MIN = 1e-3
S5_DT_MAX = 1e-1
MLA_NOPE = 128
MLA_ROPE = 64
MLA_V = 128
MLA_HEADS = TOK_WIDTH // MLA_V
MLA_Q_RANK = 512
MLA_KV_RANK = 512
ROPE_THETA = 10000.0
D_FF = 5632
Q_BLOCK = 128
EPS = 1e-6

kernel_name = "yoco_s5_mla_macaron_memory_trunk"


def rmsnorm(x, g):
    xf = x.astype(jnp.float32)
    y = xf * lax.rsqrt(jnp.mean(xf * xf, axis=-1, keepdims=True) + EPS)
    return (y * g.astype(jnp.float32)).astype(x.dtype)


def swiglu(h, w_gate, w_up, w_down):
    return (jax.nn.silu(h @ w_gate) * (h @ w_up)) @ w_down


def rope_tables(positions):
    inv_freq = ROPE_THETA ** (-jnp.arange(0, MLA_ROPE, 2, dtype=jnp.float32) / MLA_ROPE)
    ang = positions.astype(jnp.float32)[..., None] * inv_freq
    return jnp.cos(ang), jnp.sin(ang)


def apply_rope(t, cos, sin):
    half = t.shape[-1] // 2
    tf = t.astype(jnp.float32)
    t1, t2 = tf[..., :half], tf[..., half:]
    return jnp.concatenate([t1 * cos - t2 * sin, t1 * sin + t2 * cos], axis=-1).astype(t.dtype)


def s5_mix(u, lam_re, lam_im, b_re, b_im, c_re, c_im, d, log_dt, w_glu, b_glu):
    bsz, seq, _ = u.shape
    f32 = jnp.float32
    uf = u.astype(f32).reshape(bsz, seq, S5_GROUPS, S5_GROUP)
    lam = lax.complex(lam_re.astype(f32), lam_im.astype(f32))
    dt = jnp.exp(log_dt.astype(f32))[:, None]
    lam_bar = jnp.exp(lam * dt)
    b = lax.complex(b_re.astype(f32), b_im.astype(f32))
    b_bar = ((lam_bar - 1.0) / lam)[..., None] * b
    bu = jnp.einsum('gpc,bsgc->bsgp', b_bar, uf.astype(jnp.complex64))
    a = jnp.broadcast_to(lam_bar, bu.shape)

    def combine(left, right):
        a_l, b_l = left
        a_r, b_r = right
        return a_r * a_l, a_r * b_l + b_r

    _, states = lax.associative_scan(combine, (a, bu), axis=1)
    c = lax.complex(c_re.astype(f32), c_im.astype(f32))
    y = jnp.real(jnp.einsum('gcp,bsgp->bsgc', c, states)) + d.astype(f32).reshape(S5_GROUPS, S5_GROUP) * uf
    y = jax.nn.gelu(y.reshape(bsz, seq, TOK_WIDTH))
    y = y * jax.nn.sigmoid(y @ w_glu.astype(f32) + b_glu.astype(f32))
    return y.astype(u.dtype)


def shared_latent_kv(x, kv_in_norm, w_dkv, kv_norm, w_uk, w_uv, w_kr, cos, sin):
    bsz, seq, _ = x.shape
    h = rmsnorm(x, kv_in_norm)
    c_kv = rmsnorm(h @ w_dkv, kv_norm)
    k_nope = (c_kv @ w_uk).reshape(bsz, seq, MLA_HEADS, MLA_NOPE)
    v = (c_kv @ w_uv).reshape(bsz, seq, MLA_HEADS, MLA_V)
    k_rope = apply_rope(h @ w_kr, cos, sin)
    return k_nope, k_rope, v


def mla_attend(q_nope, q_rope, k_nope, k_rope, v):
    bsz, seq = q_nope.shape[0], q_nope.shape[1]
    scale = (MLA_NOPE + MLA_ROPE) ** -0.5
    outs = []
    for start in range(0, seq, Q_BLOCK):
        end = start + Q_BLOCK
        s = (jnp.einsum('bqhd,bkhd->bhqk', q_nope[:, start:end], k_nope[:, :end])
             + jnp.einsum('bqhr,bkr->bhqk', q_rope[:, start:end], k_rope[:, :end]))
        s = s.astype(jnp.float32) * scale
        qi = jnp.arange(start, end)[:, None]
        ki = jnp.arange(end)[None, :]
        s = jnp.where(ki <= qi, s, -jnp.inf)
        p = jax.nn.softmax(s, axis=-1).astype(v.dtype)
        outs.append(jnp.einsum('bhqk,bkhd->bqhd', p, v[:, :end]))
    o = jnp.concatenate(outs, axis=1)
    return o.reshape(bsz, seq, MLA_HEADS * MLA_V)


def mem_attend(q, mem_k, mem_v):
    bsz, seq, _ = q.shape
    qh = q.reshape(bsz, seq, MEM_HEADS, MEM_HEAD_DIM)
    s = jnp.einsum('bqhd,bkhd->bhqk', qh, mem_k).astype(jnp.float32) * (MEM_HEAD_DIM ** -0.5)
    p = jax.nn.softmax(s, axis=-1).astype(mem_v.dtype)
    return jnp.einsum('bhqk,bkhd->bqhd', p, mem_v).reshape(bsz, seq, MEM_WIDTH)


def setup_inputs(seed: int = 0) -> dict:
    key = jax.random.key(seed)
    ks = iter(jax.random.split(key, 40))
    f32 = jnp.float32

    def nrm(shape, fan_in):
        return jax.random.normal(next(ks), shape, f32) * (fan_in ** -0.5)

    def gain(shape):
        return 1.0 + 0.02 * jax.random.normal(next(ks), shape, f32)

    x = jax.random.normal(next(ks), (BATCH, SEQ, D_MODEL), f32)
    mem = jax.random.normal(next(ks), (BATCH, N_MEM, D_MODEL), f32)
    offset = jax.random.randint(next(ks), (BATCH, 1), 0, 1024, dtype=jnp.int32)
    positions = offset + jnp.arange(SEQ, dtype=jnp.int32)[None, :]

    n_idx = jnp.arange(S5_STATE, dtype=f32)[None, None, :]
    lam_re = -0.5 + 0.01 * jax.random.normal(next(ks), (N_A_LAYERS, S5_GROUPS, S5_STATE), f32)
    lam_im = math.pi * n_idx + 0.01 * jax.random.normal(next(ks), (N_A_LAYERS, S5_GROUPS, S5_STATE), f32)
    log_dt = jax.random.uniform(next(ks), (N_A_LAYERS, S5_GROUPS), f32,
                                math.log(S5_DT_MIN), math.log(S5_DT_MAX))

    return {
        'x': x,
        'mem': mem,
        'positions': positions,
        'norms': gain((DEPTH, 6, D_MODEL)),
        'ffn_w_gate': nrm((DEPTH, 2, D_MODEL, D_FF), D_MODEL),
        'ffn_w_up': nrm((DEPTH, 2, D_MODEL, D_FF), D_MODEL),
        'ffn_w_down': nrm((DEPTH, 2, D_FF, D_MODEL), D_FF),
        'w_out': nrm((DEPTH, MIX_WIDTH, D_MODEL), MIX_WIDTH),
        'mem_norm': gain((DEPTH, D_MODEL)),
        'mem_w_kv': nrm((DEPTH, D_MODEL, 2 * MEM_WIDTH), D_MODEL),
        'a_w_in': nrm((N_A_LAYERS, D_MODEL, TOK_WIDTH + MEM_WIDTH), D_MODEL),
        's5_lambda_re': lam_re,
        's5_lambda_im': lam_im,
        's5_b_re': nrm((N_A_LAYERS, S5_GROUPS, S5_STATE, S5_GROUP), 2 * S5_GROUP),
        's5_b_im': nrm((N_A_LAYERS, S5_GROUPS, S5_STATE, S5_GROUP), 2 * S5_GROUP),
        's5_c_re': nrm((N_A_LAYERS, S5_GROUPS, S5_GROUP, S5_STATE), 2 * S5_STATE),
        's5_c_im': nrm((N_A_LAYERS, S5_GROUPS, S5_GROUP, S5_STATE), 2 * S5_STATE),
        's5_d': jax.random.normal(next(ks), (N_A_LAYERS, TOK_WIDTH), f32),
        's5_log_dt': log_dt,
        's5_w_glu': nrm((N_A_LAYERS, TOK_WIDTH, TOK_WIDTH), TOK_WIDTH),
        's5_b_glu': 0.01 * jax.random.normal(next(ks), (N_A_LAYERS, TOK_WIDTH), f32),
        'b_w_in': nrm((N_B_LAYERS, D_MODEL, MLA_Q_RANK + MEM_WIDTH), D_MODEL),
        'mla_q_norm': gain((N_B_LAYERS, MLA_Q_RANK)),
        'mla_w_uq': nrm((N_B_LAYERS, MLA_Q_RANK, MLA_HEADS * (MLA_NOPE + MLA_ROPE)), MLA_Q_RANK),
        'kv_in_norm': gain((D_MODEL,)),
        'w_dkv': nrm((D_MODEL, MLA_KV_RANK), D_MODEL),
        'kv_norm': gain((MLA_KV_RANK,)),
        'w_uk': nrm((MLA_KV_RANK, MLA_HEADS * MLA_NOPE), MLA_KV_RANK),
        'w_uv': nrm((MLA_KV_RANK, MLA_HEADS * MLA_V), MLA_KV_RANK),
        'w_kr': nrm((D_MODEL, MLA_ROPE), D_MODEL),
    }


def reference(x, mem, positions, norms, ffn_w_gate, ffn_w_up, ffn_w_down, w_out,
              mem_norm, mem_w_kv, a_w_in, s5_lambda_re, s5_lambda_im, s5_b_re, s5_b_im,
              s5_c_re, s5_c_im, s5_d, s5_log_dt, s5_w_glu, s5_b_glu, b_w_in, mla_q_norm,
              mla_w_uq, kv_in_norm, w_dkv, kv_norm, w_uk, w_uv, w_kr):
    bsz, seq, _ = x.shape
    n_mem = mem.shape[1]
    cos, sin = rope_tables(positions)
    cos_h, sin_h = cos[:, :, None, :], sin[:, :, None, :]
    k_nope = k_rope = v_shared = None

    for l in range(DEPTH):
        if l == N_A_LAYERS:
            k_nope, k_rope, v_shared = shared_latent_kv(x, kv_in_norm, w_dkv, kv_norm,
                                                        w_uk, w_uv, w_kr, cos, sin)
        g = norms[l]
        h = rmsnorm(x, g[0])
        x = x + 0.5 * rmsnorm(swiglu(h, ffn_w_gate[l, 0], ffn_w_up[l, 0], ffn_w_down[l, 0]), g[1])

        mkv = rmsnorm(mem, mem_norm[l]) @ mem_w_kv[l]
        mem_k = mkv[..., :MEM_WIDTH].reshape(bsz, n_mem, MEM_HEADS, MEM_HEAD_DIM)
        mem_v = mkv[..., MEM_WIDTH:].reshape(bsz, n_mem, MEM_HEADS, MEM_HEAD_DIM)

        h = rmsnorm(x, g[2])
        if l < N_A_LAYERS:
            z = h @ a_w_in[l]
            tok = s5_mix(z[..., :TOK_WIDTH], s5_lambda_re[l], s5_lambda_im[l], s5_b_re[l],
                         s5_b_im[l], s5_c_re[l], s5_c_im[l], s5_d[l], s5_log_dt[l],
                         s5_w_glu[l], s5_b_glu[l])
            q_mem = z[..., TOK_WIDTH:]
        else:
            j = l - N_A_LAYERS
            z = h @ b_w_in[j]
            c_q = rmsnorm(z[..., :MLA_Q_RANK], mla_q_norm[j])
            q = (c_q @ mla_w_uq[j]).reshape(bsz, seq, MLA_HEADS, MLA_NOPE + MLA_ROPE)
            q_nope = q[..., :MLA_NOPE]
            q_rope = apply_rope(q[..., MLA_NOPE:], cos_h, sin_h)
            tok = mla_attend(q_nope, q_rope, k_nope, k_rope, v_shared)
            q_mem = z[..., MLA_Q_RANK:]
        mem_o = mem_attend(q_mem, mem_k, mem_v)
        o = jnp.concatenate([tok, mem_o], axis=-1) @ w_out[l]
        x = x + rmsnorm(o, g[3])

        h = rmsnorm(x, g[4])
        x = x + 0.5 * rmsnorm(swiglu(h, ffn_w_gate[l, 1], ffn_w_up[l, 1], ffn_w_down[l, 1]), g[5])
    return x
```

```python
import functools
import math

import jax
import jax.numpy as jnp
from jax import lax
from jax.experimental import pallas as pl
from jax.experimental.pallas import tpu as pltpu

F32 = jnp.float32
BF16 = jnp.bfloat16

EPS = 1e-6
ROPE_THETA = 10000.0
LOG2E = 1.4426950408889634

V7X_LANES = 128
V7X_VMEM_BYTES = 64 * 1024 * 1024
VMEM_LIMIT = V7X_VMEM_BYTES - 6 * 1024 * 1024

S5_GROUP = 16
MLA_NOPE = 128
MLA_ROPE = 64
MLA_V = 128
MEM_HEADS = 4
MEM_HEAD_DIM = 128
MEM_WIDTH = MEM_HEADS * MEM_HEAD_DIM

S5_CHUNK = 8
S5_GROUPS_PER_TILE = V7X_LANES // S5_GROUP
Q_SLOT = 2 * V7X_LANES


def _tile(n, want):
    t = min(n, want)
    assert n % t == 0, (n, want)
    return t


def _cparams(sem):
    return pltpu.CompilerParams(dimension_semantics=sem, vmem_limit_bytes=VMEM_LIMIT)


def _const_spec(shape, index_map):
    return pl.BlockSpec(shape, index_map, pipeline_mode=pl.Buffered(1))


def _rms(xf, g):
    ms = jnp.mean(xf * xf, axis=-1, keepdims=True)
    return xf * lax.rsqrt(ms + EPS) * g


def _dot(a, b):
    return jnp.dot(a, b, preferred_element_type=F32)


def _dot_nt(a, b):
    return lax.dot_general(a, b, (((1,), (1,)), ((), ())), preferred_element_type=F32)


FFN_ROW_CHUNK = 256


def _ffn_kernel(x_ref, g0_ref, g1_ref, wg_ref, wu_ref, wd_ref, o_ref, h_ref):
    f = pl.program_id(1)
    rc = min(FFN_ROW_CHUNK, x_ref.shape[0])
    n_chunks = x_ref.shape[0] // rc

    def for_row_chunks(body):
        def step(r, carry):
            body(pl.ds(pl.multiple_of(r * rc, rc), rc))
            return carry
        lax.fori_loop(0, n_chunks, step, 0)

    @pl.when(f == 0)
    def _():
        def init(rows):
            h_ref[rows, :] = _rms(x_ref[rows, :], g0_ref[...]).astype(BF16)
            o_ref[rows, :] = jnp.zeros((rc, o_ref.shape[1]), F32)
        for_row_chunks(init)

    def accumulate(rows):
        h = h_ref[rows, :]
        g = _dot(h, wg_ref[...])
        u = _dot(h, wu_ref[...])
        a = (g * jax.nn.sigmoid(g) * u).astype(BF16)
        o_ref[rows, :] += _dot(a, wd_ref[...])
    for_row_chunks(accumulate)

    @pl.when(f == pl.num_programs(1) - 1)
    def _():
        def finish(rows):
            o_ref[rows, :] = x_ref[rows, :] + 0.5 * _rms(o_ref[rows, :], g1_ref[...])
        for_row_chunks(finish)


def _ffn(x2, norms, wg, wu, wd, layer, half):
    t, d = x2.shape
    ff = wg.shape[-1]
    tm = _tile(t, 1024)
    tf = _tile(ff, 512)
    n0, n1 = (0, 1) if half == 0 else (4, 5)
    return pl.pallas_call(
        _ffn_kernel,
        grid=(t // tm, ff // tf),
        in_specs=[
            pl.BlockSpec((tm, d), lambda i, f: (i, 0)),
            pl.BlockSpec((None, None, 1, d), lambda i, f: (layer, n0, 0, 0)),
            pl.BlockSpec((None, None, 1, d), lambda i, f: (layer, n1, 0, 0)),
            pl.BlockSpec((None, None, d, tf), lambda i, f: (layer, half, 0, f)),
            pl.BlockSpec((None, None, d, tf), lambda i, f: (layer, half, 0, f)),
            pl.BlockSpec((None, None, tf, d), lambda i, f: (layer, half, f, 0)),
        ],
        out_specs=pl.BlockSpec((tm, d), lambda i, f: (i, 0)),
        out_shape=jax.ShapeDtypeStruct((t, d), F32),
        scratch_shapes=[pltpu.VMEM((tm, d), BF16)],
        compiler_params=_cparams(("parallel", "arbitrary")),
        name="ffn",
    )(x2, norms, norms, wg, wu, wd)


def _mem_kv_kernel(mem_ref, g_ref, w_ref, o_ref):
    h = _rms(mem_ref[...], g_ref[...]).astype(BF16)
    o_ref[...] = _dot(h, w_ref[...]).astype(BF16)


def _mem_kv(mem2, mem_norm, w_kv):
    depth, d, n = w_kv.shape
    rows = mem2.shape[0]
    return pl.pallas_call(
        _mem_kv_kernel,
        grid=(depth,),
        in_specs=[
            pl.BlockSpec((rows, d), lambda l: (0, 0)),
            pl.BlockSpec((None, 1, d), lambda l: (l, 0, 0)),
            pl.BlockSpec((None, d, n), lambda l: (l, 0, 0)),
        ],
        out_specs=pl.BlockSpec((None, rows, n), lambda l: (l, 0, 0)),
        out_shape=jax.ShapeDtypeStruct((depth, rows, n), BF16),
        compiler_params=_cparams(("parallel",)),
        name="mem_kv",
    )(mem2, mem_norm, w_kv)


def _mem_attend(qm, mkv):
    scale = MEM_HEAD_DIM ** -0.5
    outs = []
    for hd in range(MEM_HEADS):
        lo = hd * MEM_HEAD_DIM
        qh = qm[:, lo:lo + MEM_HEAD_DIM]
        kh = mkv[:, lo:lo + MEM_HEAD_DIM]
        vh = mkv[:, MEM_WIDTH + lo:MEM_WIDTH + lo + MEM_HEAD_DIM]
        s = _dot_nt(qh, kh) * scale
        e = jnp.exp(s - jnp.max(s, axis=-1, keepdims=True))
        p = e / jnp.sum(e, axis=-1, keepdims=True)
        outs.append(_dot(p.astype(BF16), vh).astype(BF16))
    return jnp.concatenate(outs, axis=-1)


def _a_in_kernel(x_ref, g_ref, w_ref, u_ref, qm_ref):
    h = _rms(x_ref[...], g_ref[...]).astype(BF16)
    z = _dot(h, w_ref[...])
    tok = u_ref.shape[-1]
    u_ref[...] = z[:, :tok].astype(BF16)
    qm_ref[...] = z[:, tok:].astype(BF16)


def _a_in(x2, norms, w_in, layer):
    t, d = x2.shape
    n = w_in.shape[-1]
    tok = n - MEM_WIDTH
    tm = _tile(t, 512)
    return pl.pallas_call(
        _a_in_kernel,
        grid=(t // tm,),
        in_specs=[
            pl.BlockSpec((tm, d), lambda i: (i, 0)),
            _const_spec((None, None, 1, d), lambda i: (layer, 2, 0, 0)),
            _const_spec((None, d, n), lambda i: (layer, 0, 0)),
        ],
        out_specs=[
            pl.BlockSpec((tm, tok), lambda i: (i, 0)),
            pl.BlockSpec((tm, MEM_WIDTH), lambda i: (i, 0)),
        ],
        out_shape=[
            jax.ShapeDtypeStruct((t, tok), BF16),
            jax.ShapeDtypeStruct((t, MEM_WIDTH), BF16),
        ],
        compiler_params=_cparams(("parallel",)),
        name="a_in",
    )(x2, norms, w_in)


def _s5_prep(lam_re, lam_im, b_re, b_im, c_re, c_im, d, log_dt, n_chunks):
    L = S5_CHUNK
    g, p = lam_re.shape
    c = b_re.shape[-1]
    g8 = S5_GROUPS_PER_TILE
    nsg = g // g8
    hp = lax.Precision.HIGHEST
    lam = lax.complex(lam_re, lam_im)
    dt = jnp.exp(log_dt)[:, None]
    z = lam * dt
    lam_bar = jnp.exp(z)
    b_bar = ((lam_bar - 1.0) / lam)[..., None] * lax.complex(b_re, b_im)
    cc = lax.complex(c_re, c_im)
    taus = jnp.arange(L + 1, dtype=F32)
    pw = jnp.exp(z[None] * taus[:, None, None].astype(jnp.complex64))

    def _cplx_einsum(spec, a, b):
        ar, ai, br, bi = jnp.real(a), jnp.imag(a), jnp.real(b), jnp.imag(b)
        e = functools.partial(jnp.einsum, spec, precision=hp)
        return e(ar, br) - e(ai, bi), e(ar, bi) + e(ai, br)

    cpw = cc[None] * pw[:, :, None, :]
    k_re, _ = _cplx_einsum('tgcp,gpd->tgcd', cpw[:L], b_bar)
    eye_c = jnp.eye(c, dtype=F32)
    k_re = k_re.at[0].add(eye_c[None] * d.reshape(g, c)[:, :, None])

    eye8 = jnp.eye(g8, dtype=F32)
    idx = jnp.arange(L)[None, :] - jnp.arange(L)[:, None]
    kst = k_re[jnp.clip(idx, 0, L - 1)] * (idx >= 0)[:, :, None, None, None].astype(F32)
    kst = kst.reshape(L, L, nsg, g8, c, c)
    toep = jnp.einsum('stxgcd,gh->xsgdthc', kst, eye8).reshape(nsg, L * 128, L * 128)

    w_in = pw[L - 1 - jnp.arange(L)][:, :, :, None] * b_bar[None]
    w_in = w_in.reshape(L, nsg, g8, p, c)

    def _bin(part):
        return jnp.einsum('sxgpd,gh->xsgdhp', part, eye8).reshape(nsg, L * 128, g8 * p)

    bin_m = jnp.concatenate([_bin(jnp.real(w_in)), _bin(jnp.imag(w_in))], axis=-1)

    v_out = cpw[1:L + 1].reshape(L, nsg, g8, c, p)

    def _cout(part):
        return jnp.einsum('txgcp,gh->xgpthc', part, eye8).reshape(nsg, g8 * p, L * 128)

    cout_m = jnp.concatenate([_cout(jnp.real(v_out)), _cout(-jnp.imag(v_out))], axis=1)

    nsteps = max(1, int(math.ceil(math.log2(n_chunks))))
    steps = (L * 2.0 ** jnp.arange(nsteps, dtype=F32))
    lp = jnp.exp(z[None] * steps[:, None, None].astype(jnp.complex64))
    lp = lp.reshape(nsteps, nsg, g8 * p)
    pows = jnp.stack([jnp.real(lp), jnp.imag(lp)], axis=1)
    pows = pows.transpose(2, 0, 1, 3).reshape(nsg, 2 * nsteps, g8 * p)
    return toep.astype(BF16), bin_m.astype(BF16), cout_m.astype(BF16), pows.astype(F32)


def _s5_kernel(u_ref, toep_ref, bin_ref, cout_ref, pw_ref, y_ref):
    L = S5_CHUNK
    nj = u_ref.shape[1]
    half = bin_ref.shape[-1] // 2
    uu = jnp.concatenate([u_ref[s] for s in range(L)], axis=1)
    e = _dot(uu, bin_ref[...])
    er, ei = e[:, :half], e[:, half:]
    row = lax.broadcasted_iota(jnp.int32, (nj, half), 0)

    def shifted(v, dd):
        return jnp.where(row >= dd, pltpu.roll(v, dd, 0), 0.0)

    nsteps = pw_ref.shape[0] // 2
    for k in range(nsteps):
        dd = 1 << k
        if dd >= nj:
            break
        lr = pw_ref[2 * k:2 * k + 1, :]
        li = pw_ref[2 * k + 1:2 * k + 2, :]
        sr, si = shifted(er, dd), shifted(ei, dd)
        er, ei = er + (lr * sr - li * si), ei + (lr * si + li * sr)
    xin = jnp.concatenate([shifted(er, 1), shifted(ei, 1)], axis=1).astype(BF16)
    y = _dot(uu, toep_ref[...]) + _dot(xin, cout_ref[...])
    for t in range(L):
        y_ref[t] = y[:, t * 128:(t + 1) * 128].astype(BF16)


def _s5(u2, prep, batch):
    toep, bin_m, cout_m, pows = prep
    t, tok = u2.shape
    L = S5_CHUNK
    nj = t // (batch * L)
    nsg = tok // V7X_LANES
    u4 = u2.reshape(batch, L, nj, tok)
    kk = L * V7X_LANES
    y4 = pl.pallas_call(
        _s5_kernel,
        grid=(nsg, batch),
        in_specs=[
            pl.BlockSpec((None, L, nj, V7X_LANES), lambda x, b: (b, 0, 0, x)),
            pl.BlockSpec((None, kk, kk), lambda x, b: (x, 0, 0)),
            pl.BlockSpec((None, kk, bin_m.shape[-1]), lambda x, b: (x, 0, 0)),
            pl.BlockSpec((None, cout_m.shape[1], kk), lambda x, b: (x, 0, 0)),
            pl.BlockSpec((None, pows.shape[1], pows.shape[2]), lambda x, b: (x, 0, 0)),
        ],
        out_specs=pl.BlockSpec((None, L, nj, V7X_LANES), lambda x, b: (b, 0, 0, x)),
        out_shape=jax.ShapeDtypeStruct((batch, L, nj, tok), BF16),
        compiler_params=_cparams(("parallel", "parallel")),
        name="s5",
    )(u4, toep, bin_m, cout_m, pows)
    return y4.reshape(t, tok)


def _mix_tail(x_ref, tok_bf, qm_ref, mkv_ref, wo_ref, g_ref, o_ref):
    tok = tok_bf.shape[-1]
    mem_o = _mem_attend(qm_ref[...], mkv_ref[...])
    o = _dot(tok_bf, wo_ref[:tok, :]) + _dot(mem_o, wo_ref[tok:, :])
    o_ref[...] = x_ref[...] + _rms(o, g_ref[...])


def _mix_out_a_kernel(x_ref, y_ref, qm_ref, mkv_ref, wglu_ref, bglu_ref, wo_ref, g_ref, o_ref):
    y = jax.nn.gelu(y_ref[...].astype(F32))
    gate = jax.nn.sigmoid(_dot(y.astype(BF16), wglu_ref[...]) + bglu_ref[...])
    _mix_tail(x_ref, (y * gate).astype(BF16), qm_ref, mkv_ref, wo_ref, g_ref, o_ref)


def _mix_out_b_kernel(x_ref, a_ref, qm_ref, mkv_ref, wo_ref, g_ref, o_ref):
    _mix_tail(x_ref, a_ref[...], qm_ref, mkv_ref, wo_ref, g_ref, o_ref)


def _mix_out(x2, tok_in, qm, mkv_all, norms, w_out, layer, seq, glu=None):
    t, d = x2.shape
    tok = tok_in.shape[-1]
    tm = _tile(t, 512)
    tiles_per_seq = seq // tm
    nm = mkv_all.shape[1] // (t // seq)
    row = lambda i: (i, 0)
    specs = [
        pl.BlockSpec((tm, d), row),
        pl.BlockSpec((tm, tok), row),
        pl.BlockSpec((tm, MEM_WIDTH), row),
        pl.BlockSpec((None, nm, 2 * MEM_WIDTH), lambda i: (layer, i // tiles_per_seq, 0)),
    ]
    args = [x2, tok_in, qm, mkv_all]
    if glu is not None:
        w_glu, b_glu, j = glu
        specs += [
            _const_spec((None, tok, tok), lambda i: (j, 0, 0)),
            _const_spec((None, 1, tok), lambda i: (j, 0, 0)),
        ]
        args += [w_glu, b_glu]
    specs += [
        _const_spec((None, d, d), lambda i: (layer, 0, 0)),
        _const_spec((None, None, 1, d), lambda i: (layer, 3, 0, 0)),
    ]
    args += [w_out, norms]
    return pl.pallas_call(
        _mix_out_a_kernel if glu is not None else _mix_out_b_kernel,
        grid=(t // tm,),
        in_specs=specs,
        out_specs=pl.BlockSpec((tm, d), row),
        out_shape=jax.ShapeDtypeStruct((t, d), F32),
        compiler_params=_cparams(("parallel",)),
        name="mix_out_a" if glu is not None else "mix_out_b",
    )(*args)


def _kv_kernel(x_ref, pos_ref, frq_ref, sgn_ref, gin_ref, wd_ref, gkv_ref, wuk_ref, wuv_ref, wkr_ref,
               k_ref, v_ref, rc_ref, rs_ref):
    ang = pos_ref[...].astype(F32) * frq_ref[...]
    rc = jnp.cos(ang)
    rs = jnp.sin(ang) * sgn_ref[...]
    rc_ref[...] = rc
    rs_ref[...] = rs
    h = _rms(x_ref[...], gin_ref[...]).astype(BF16)
    ckv = _rms(_dot(h, wd_ref[...]), gkv_ref[...]).astype(BF16)
    kn = _dot(ckv, wuk_ref[...])
    v_ref[...] = _dot(ckv, wuv_ref[...]).astype(BF16)
    kr2 = _dot(h, wkr_ref[...])
    kr = (kr2[:, :V7X_LANES] * rc + kr2[:, V7X_LANES:] * rs).astype(BF16)
    heads = k_ref.shape[-1] // Q_SLOT
    for hh in range(heads):
        k_ref[:, hh * Q_SLOT:hh * Q_SLOT + MLA_NOPE] = kn[:, hh * MLA_NOPE:(hh + 1) * MLA_NOPE].astype(BF16)
        k_ref[:, hh * Q_SLOT + MLA_NOPE:(hh + 1) * Q_SLOT] = kr


def _kv(x2, pos2, frq, sgn, g_in, w_dkv, g_kv, w_uk, w_uv, w_kr2):
    t, d = x2.shape
    r = w_dkv.shape[-1]
    hv = w_uv.shape[-1]
    heads = hv // MLA_V
    tm = _tile(t, 512)
    row = lambda i: (i, 0)
    c2 = lambda i: (0, 0)
    return pl.pallas_call(
        _kv_kernel,
        grid=(t // tm,),
        in_specs=[
            pl.BlockSpec((tm, d), row),
            pl.BlockSpec((tm, 1), row),
            _const_spec((1, V7X_LANES), c2),
            _const_spec((1, V7X_LANES), c2),
            _const_spec((1, d), c2),
            _const_spec((d, r), c2),
            _const_spec((1, r), c2),
            _const_spec((r, heads * MLA_NOPE), c2),
            _const_spec((r, hv), c2),
            _const_spec((d, 2 * V7X_LANES), c2),
        ],
        out_specs=[
            pl.BlockSpec((tm, heads * Q_SLOT), row),
            pl.BlockSpec((tm, hv), row),
            pl.BlockSpec((tm, V7X_LANES), row),
            pl.BlockSpec((tm, V7X_LANES), row),
        ],
        out_shape=[
            jax.ShapeDtypeStruct((t, heads * Q_SLOT), BF16),
            jax.ShapeDtypeStruct((t, hv), BF16),
            jax.ShapeDtypeStruct((t, V7X_LANES), F32),
            jax.ShapeDtypeStruct((t, V7X_LANES), F32),
        ],
        compiler_params=_cparams(("parallel",)),
        name="kv",
    )(x2, pos2, frq, sgn, g_in, w_dkv, g_kv, w_uk, w_uv, w_kr2)


def _b_in_kernel(x_ref, rc_ref, rs_ref, g_ref, w_ref, gq_ref, wq_ref, q_ref, qm_ref):
    h = _rms(x_ref[...], g_ref[...]).astype(BF16)
    z = _dot(h, w_ref[...])
    rank = gq_ref.shape[-1]
    qm_ref[...] = z[:, rank:].astype(BF16)
    cq = _rms(z[:, :rank], gq_ref[...]).astype(BF16)
    zq = _dot(cq, wq_ref[...])
    heads = q_ref.shape[-1] // Q_SLOT
    rc, rs = rc_ref[...], rs_ref[...]
    sw0 = heads * Q_SLOT
    for hh in range(heads):
        lo = hh * Q_SLOT
        q_ref[:, lo:lo + MLA_NOPE] = zq[:, lo:lo + MLA_NOPE].astype(BF16)
        main = zq[:, lo + MLA_NOPE:lo + Q_SLOT]
        swp = zq[:, sw0 + hh * V7X_LANES:sw0 + (hh + 1) * V7X_LANES]
        q_ref[:, lo + MLA_NOPE:lo + Q_SLOT] = (main * rc + swp * rs).astype(BF16)


def _b_in(x2, rc, rs, norms, w_in, g_q, w_q, layer, j):
    t, d = x2.shape
    n = w_in.shape[-1]
    rank = g_q.shape[-1]
    nq = w_q.shape[-1]
    heads = nq // (Q_SLOT + V7X_LANES)
    tm = _tile(t, 512)
    row = lambda i: (i, 0)
    return pl.pallas_call(
        _b_in_kernel,
        grid=(t // tm,),
        in_specs=[
            pl.BlockSpec((tm, d), row),
            pl.BlockSpec((tm, V7X_LANES), row),
            pl.BlockSpec((tm, V7X_LANES), row),
            _const_spec((None, None, 1, d), lambda i: (layer, 2, 0, 0)),
            _const_spec((None, d, n), lambda i: (j, 0, 0)),
            _const_spec((None, 1, rank), lambda i: (j, 0, 0)),
            _const_spec((None, rank, nq), lambda i: (j, 0, 0)),
        ],
        out_specs=[
            pl.BlockSpec((tm, heads * Q_SLOT), row),
            pl.BlockSpec((tm, MEM_WIDTH), row),
        ],
        out_shape=[
            jax.ShapeDtypeStruct((t, heads * Q_SLOT), BF16),
            jax.ShapeDtypeStruct((t, MEM_WIDTH), BF16),
        ],
        compiler_params=_cparams(("parallel",)),
        name="b_in",
    )(x2, rc, rs, norms, w_in, g_q, w_q)


def _attn_kernel(q_ref, k_ref, v_ref, o_ref, *, blk, scale):
    i = pl.program_id(2)
    q = q_ref[...]
    c = scale * LOG2E

    def step(j, carry, diagonal):
        m, l, acc = carry
        start = pl.multiple_of(j * blk, blk)
        ks = k_ref[pl.ds(start, blk), :]
        vs = v_ref[pl.ds(start, blk), :]
        s = _dot_nt(q, ks)
        if diagonal:
            rows = lax.broadcasted_iota(jnp.int32, s.shape, 0)
            cols = lax.broadcasted_iota(jnp.int32, s.shape, 1)
            s = jnp.where(cols <= rows, s, -jnp.inf)
        m_new = jnp.maximum(m, jnp.max(s, axis=-1, keepdims=True))
        alpha = jnp.exp2((m - m_new) * c)
        p = jnp.exp2((s - m_new) * c)
        l = alpha * l + jnp.sum(p, axis=-1, keepdims=True)
        acc = alpha * acc + _dot(p.astype(BF16), vs)
        return m_new, l, acc

    init = (jnp.full((blk, 1), -jnp.inf, F32), jnp.zeros((blk, 1), F32),
            jnp.zeros((blk, v_ref.shape[-1]), F32))
    carry = lax.fori_loop(0, i, lambda j, cr: step(j, cr, False), init)
    _, l, acc = step(i, carry, True)
    o_ref[...] = (acc / l).astype(BF16)


def _attn(q, k_all, v, batch, seq):
    t = q.shape[0]
    heads = v.shape[-1] // MLA_V
    blk = _tile(seq, 512)
    nq = seq // blk
    scale = (MLA_NOPE + MLA_ROPE) ** -0.5
    return pl.pallas_call(
        functools.partial(_attn_kernel, blk=blk, scale=scale),
        grid=(batch, heads, nq),
        in_specs=[
            pl.BlockSpec((blk, Q_SLOT), lambda b, h, i: (b * nq + i, h)),
            pl.BlockSpec((seq, Q_SLOT), lambda b, h, i: (b, h)),
            pl.BlockSpec((seq, MLA_V), lambda b, h, i: (b, h)),
        ],
        out_specs=pl.BlockSpec((blk, MLA_V), lambda b, h, i: (b * nq + i, h)),
        out_shape=jax.ShapeDtypeStruct((t, heads * MLA_V), BF16),
        compiler_params=_cparams(("parallel", "parallel", "arbitrary")),
        name="attn",
    )(q, k_all, v)


def _pad_cols(w, n):
    return jnp.pad(w, ((0, 0),) * (w.ndim - 1) + ((0, n - w.shape[-1]),))


def _q_weight(w_uq):
    nb, r, n = w_uq.shape
    heads = n // (MLA_NOPE + MLA_ROPE)
    w = w_uq.reshape(nb, r, heads, MLA_NOPE + MLA_ROPE)
    half = MLA_ROPE // 2
    nope, t1, t2 = w[..., :MLA_NOPE], w[..., MLA_NOPE:MLA_NOPE + half], w[..., MLA_NOPE + half:]
    main = _pad_cols(jnp.concatenate([nope, t1, t2], axis=-1), Q_SLOT).reshape(nb, r, heads * Q_SLOT)
    swp = _pad_cols(jnp.concatenate([t2, t1], axis=-1), V7X_LANES).reshape(nb, r, heads * V7X_LANES)
    return jnp.concatenate([main, swp], axis=-1)


def kernel(x, mem, positions, norms, ffn_w_gate, ffn_w_up, ffn_w_down, w_out, mem_norm, mem_w_kv, a_w_in, s5_lambda_re, s5_lambda_im, s5_b_re, s5_b_im, s5_c_re, s5_c_im, s5_d, s5_log_dt, s5_w_glu, s5_b_glu, b_w_in, mla_q_norm, mla_w_uq, kv_in_norm, w_dkv, kv_norm, w_uk, w_uv, w_kr):
    bsz, seq, d = x.shape
    t = bsz * seq
    depth = norms.shape[0]
    n_a = a_w_in.shape[0]
    n_b = b_w_in.shape[0]
    L = S5_CHUNK
    nj = seq // L
    half = MLA_ROPE // 2

    norms4 = norms.reshape(depth, norms.shape[1], 1, d)
    wg, wu, wd = ffn_w_gate.astype(BF16), ffn_w_up.astype(BF16), ffn_w_down.astype(BF16)
    w_out_b = w_out.astype(BF16)
    mkv_all = _mem_kv(mem.reshape(bsz * mem.shape[1], d), mem_norm.reshape(depth, 1, d),
                      mem_w_kv.astype(BF16))

    xp = x.reshape(bsz, nj, L, d).transpose(0, 2, 1, 3).reshape(t, d)
    a_w_in_b = a_w_in.astype(BF16)
    w_glu_b = s5_w_glu.astype(BF16)
    b_glu3 = s5_b_glu.reshape(n_a, 1, -1)
    for l in range(n_a):
        prep = _s5_prep(s5_lambda_re[l], s5_lambda_im[l], s5_b_re[l], s5_b_im[l], s5_c_re[l], s5_c_im[l],
                        s5_d[l], s5_log_dt[l], nj)
        xp = _ffn(xp, norms4, wg, wu, wd, l, 0)
        u, qm = _a_in(xp, norms4, a_w_in_b, l)
        y = _s5(u, prep, bsz)
        xp = _mix_out(xp, y, qm, mkv_all, norms4, w_out_b, l, seq, glu=(w_glu_b, b_glu3, l))
        xp = _ffn(xp, norms4, wg, wu, wd, l, 1)
    x2 = xp.reshape(bsz, L, nj, d).transpose(0, 2, 1, 3).reshape(t, d)

    inv_freq = ROPE_THETA ** (-jnp.arange(0, MLA_ROPE, 2, dtype=F32) / MLA_ROPE)
    zeros = jnp.zeros((V7X_LANES - MLA_ROPE,), F32)
    frq = jnp.concatenate([inv_freq, inv_freq, zeros]).reshape(1, V7X_LANES)
    sgn = jnp.concatenate([-jnp.ones((half,), F32), jnp.ones((half,), F32), zeros]).reshape(1, V7X_LANES)
    kr1, kr2 = w_kr[:, :half], w_kr[:, half:]
    w_kr2 = jnp.concatenate([_pad_cols(w_kr, V7X_LANES),
                             _pad_cols(jnp.concatenate([kr2, kr1], axis=-1), V7X_LANES)], axis=-1).astype(BF16)
    k_all, v, rc, rs = _kv(x2, positions.reshape(t, 1), frq, sgn, kv_in_norm.reshape(1, d),
                           w_dkv.astype(BF16), kv_norm.reshape(1, -1), w_uk.astype(BF16),
                           w_uv.astype(BF16), w_kr2)

    b_w_in_b = b_w_in.astype(BF16)
    w_q = _q_weight(mla_w_uq).astype(BF16)
    g_q = mla_q_norm.reshape(n_b, 1, -1)
    for j in range(n_b):
        l = n_a + j
        x2 = _ffn(x2, norms4, wg, wu, wd, l, 0)
        q, qm = _b_in(x2, rc, rs, norms4, b_w_in_b, g_q, w_q, l, j)
        o = _attn(q, k_all, v, bsz, seq)
        x2 = _mix_out(x2, o, qm, mkv_all, norms4, w_out_b, l, seq)
        x2 = _ffn(x2, norms4, wg, wu, wd, l, 1)
    return x2.reshape(bsz, seq, d)
```

```python
import functools
import math

import jax
import jax.numpy as jnp
from jax import lax
from jax.experimental import pallas as pl
from jax.experimental.pallas import tpu as pltpu

F32 = jnp.float32
BF16 = jnp.bfloat16

EPS = 1e-6
ROPE_THETA = 10000.0
LOG2E = 1.4426950408889634

V7X_LANES = 128
V7X_VMEM_BYTES = 64 * 1024 * 1024
VMEM_LIMIT = V7X_VMEM_BYTES - 6 * 1024 * 1024

S5_GROUP = 16
MLA_NOPE = 128
MLA_ROPE = 64
MLA_V = 128
MEM_HEADS = 4
MEM_HEAD_DIM = 128
MEM_WIDTH = MEM_HEADS * MEM_HEAD_DIM

S5_CHUNK = 8
S5_GROUPS_PER_TILE = V7X_LANES // S5_GROUP
Q_SLOT = 2 * V7X_LANES


def _tile(n, want):
    t = min(n, want)
    assert n % t == 0, (n, want)
    return t


def _cparams(sem):
    return pltpu.CompilerParams(dimension_semantics=sem, vmem_limit_bytes=VMEM_LIMIT)


def _const_spec(shape, index_map):
    return pl.BlockSpec(shape, index_map, pipeline_mode=pl.Buffered(1))


def _rms(xf, g):
    ms = jnp.mean(xf * xf, axis=-1, keepdims=True)
    return xf * lax.rsqrt(ms + EPS) * g


def _dot(a, b):
    return jnp.dot(a, b, preferred_element_type=F32)


def _dot_nt(a, b):
    return lax.dot_general(a, b, (((1,), (1,)), ((), ())), preferred_element_type=F32)


FFN_ROW_CHUNK = 256


def _ffn_kernel(x_ref, g0_ref, g1_ref, wg_ref, wu_ref, wd_ref, o_ref, h_ref, *, n_ff_steps):
    f = pl.program_id(1)
    rc = min(FFN_ROW_CHUNK, x_ref.shape[0])
    n_chunks = x_ref.shape[0] // rc

    def step(first, final):
        for r in range(n_chunks):
            rows = pl.ds(r * rc, rc)
            if first:
                h = _rms(x_ref[rows, :], g0_ref[...]).astype(BF16)
                h_ref[rows, :] = h
            else:
                h = h_ref[rows, :]
            g = _dot(h, wg_ref[...])
            u = _dot(h, wu_ref[...])
            a = (g * jax.nn.sigmoid(g) * u).astype(BF16)
            o = _dot(a, wd_ref[...])
            if not first:
                o = o_ref[rows, :] + o
            if final:
                o = x_ref[rows, :] + 0.5 * _rms(o, g1_ref[...])
            o_ref[rows, :] = o

    if n_ff_steps == 1:
        step(True, True)
        return
    pl.when(f == 0)(lambda: step(True, False))
    if n_ff_steps > 2:
        pl.when(jnp.logical_and(f > 0, f < n_ff_steps - 1))(lambda: step(False, False))
    pl.when(f == n_ff_steps - 1)(lambda: step(False, True))


def _ffn(x2, norms, wg, wu, wd, layer, half):
    t, d = x2.shape
    ff = wg.shape[-1]
    tm = _tile(t, 1024)
    tf = _tile(ff, 512)
    n0, n1 = (0, 1) if half == 0 else (4, 5)
    return pl.pallas_call(
        functools.partial(_ffn_kernel, n_ff_steps=ff // tf),
        grid=(t // tm, ff // tf),
        in_specs=[
            pl.BlockSpec((tm, d), lambda i, f: (i, 0)),
            pl.BlockSpec((None, None, 1, d), lambda i, f: (layer, n0, 0, 0)),
            pl.BlockSpec((None, None, 1, d), lambda i, f: (layer, n1, 0, 0)),
            pl.BlockSpec((None, None, d, tf), lambda i, f: (layer, half, 0, f)),
            pl.BlockSpec((None, None, d, tf), lambda i, f: (layer, half, 0, f)),
            pl.BlockSpec((None, None, tf, d), lambda i, f: (layer, half, f, 0)),
        ],
        out_specs=pl.BlockSpec((tm, d), lambda i, f: (i, 0)),
        out_shape=jax.ShapeDtypeStruct((t, d), F32),
        scratch_shapes=[pltpu.VMEM((tm, d), BF16)],
        compiler_params=_cparams(("parallel", "arbitrary")),
        name="ffn",
    )(x2, norms, norms, wg, wu, wd)


def _mem_kv_kernel(mem_ref, g_ref, w_ref, o_ref):
    h = _rms(mem_ref[...], g_ref[...]).astype(BF16)
    o_ref[...] = _dot(h, w_ref[...]).astype(BF16)


def _mem_kv(mem2, mem_norm, w_kv):
    depth, d, n = w_kv.shape
    rows = mem2.shape[0]
    return pl.pallas_call(
        _mem_kv_kernel,
        grid=(depth,),
        in_specs=[
            pl.BlockSpec((rows, d), lambda l: (0, 0)),
            pl.BlockSpec((None, 1, d), lambda l: (l, 0, 0)),
            pl.BlockSpec((None, d, n), lambda l: (l, 0, 0)),
        ],
        out_specs=pl.BlockSpec((None, rows, n), lambda l: (l, 0, 0)),
        out_shape=jax.ShapeDtypeStruct((depth, rows, n), BF16),
        compiler_params=_cparams(("parallel",)),
        name="mem_kv",
    )(mem2, mem_norm, w_kv)


def _mem_attend(qm, mkv):
    scale = MEM_HEAD_DIM ** -0.5
    outs = []
    for hd in range(MEM_HEADS):
        lo = hd * MEM_HEAD_DIM
        qh = qm[:, lo:lo + MEM_HEAD_DIM]
        kh = mkv[:, lo:lo + MEM_HEAD_DIM]
        vh = mkv[:, MEM_WIDTH + lo:MEM_WIDTH + lo + MEM_HEAD_DIM]
        s = _dot_nt(qh, kh) * scale
        e = jnp.exp(s - jnp.max(s, axis=-1, keepdims=True))
        p = e / jnp.sum(e, axis=-1, keepdims=True)
        outs.append(_dot(p.astype(BF16), vh).astype(BF16))
    return jnp.concatenate(outs, axis=-1)


def _a_in_kernel(x_ref, g_ref, w_ref, u_ref, qm_ref):
    h = _rms(x_ref[...], g_ref[...]).astype(BF16)
    z = _dot(h, w_ref[...])
    tok = u_ref.shape[-1]
    u_ref[...] = z[:, :tok].astype(BF16)
    qm_ref[...] = z[:, tok:].astype(BF16)


def _a_in(x2, norms, w_in, layer):
    t, d = x2.shape
    n = w_in.shape[-1]
    tok = n - MEM_WIDTH
    tm = _tile(t, 512)
    return pl.pallas_call(
        _a_in_kernel,
        grid=(t // tm,),
        in_specs=[
            pl.BlockSpec((tm, d), lambda i: (i, 0)),
            _const_spec((None, None, 1, d), lambda i: (layer, 2, 0, 0)),
            _const_spec((None, d, n), lambda i: (layer, 0, 0)),
        ],
        out_specs=[
            pl.BlockSpec((tm, tok), lambda i: (i, 0)),
            pl.BlockSpec((tm, MEM_WIDTH), lambda i: (i, 0)),
        ],
        out_shape=[
            jax.ShapeDtypeStruct((t, tok), BF16),
            jax.ShapeDtypeStruct((t, MEM_WIDTH), BF16),
        ],
        compiler_params=_cparams(("parallel",)),
        name="a_in",
    )(x2, norms, w_in)


def _s5_prep(lam_re, lam_im, b_re, b_im, c_re, c_im, d, log_dt, n_chunks):
    L = S5_CHUNK
    g, p = lam_re.shape
    c = b_re.shape[-1]
    g8 = S5_GROUPS_PER_TILE
    nsg = g // g8
    hp = lax.Precision.HIGHEST
    lam = lax.complex(lam_re, lam_im)
    dt = jnp.exp(log_dt)[:, None]
    z = lam * dt
    lam_bar = jnp.exp(z)
    b_bar = ((lam_bar - 1.0) / lam)[..., None] * lax.complex(b_re, b_im)
    cc = lax.complex(c_re, c_im)
    taus = jnp.arange(L + 1, dtype=F32)
    pw = jnp.exp(z[None] * taus[:, None, None].astype(jnp.complex64))

    def _cplx_einsum(spec, a, b):
        ar, ai, br, bi = jnp.real(a), jnp.imag(a), jnp.real(b), jnp.imag(b)
        e = functools.partial(jnp.einsum, spec, precision=hp)
        return e(ar, br) - e(ai, bi), e(ar, bi) + e(ai, br)

    cpw = cc[None] * pw[:, :, None, :]
    k_re, _ = _cplx_einsum('tgcp,gpd->tgcd', cpw[:L], b_bar)
    eye_c = jnp.eye(c, dtype=F32)
    k_re = k_re.at[0].add(eye_c[None] * d.reshape(g, c)[:, :, None])

    eye8 = jnp.eye(g8, dtype=F32)
    idx = jnp.arange(L)[None, :] - jnp.arange(L)[:, None]
    kst = k_re[jnp.clip(idx, 0, L - 1)] * (idx >= 0)[:, :, None, None, None].astype(F32)
    kst = kst.reshape(L, L, nsg, g8, c, c)
    toep = jnp.einsum('stxgcd,gh->xsgdthc', kst, eye8).reshape(nsg, L * 128, L * 128)

    w_in = pw[L - 1 - jnp.arange(L)][:, :, :, None] * b_bar[None]
    w_in = w_in.reshape(L, nsg, g8, p, c)

    def _bin(part):
        return jnp.einsum('sxgpd,gh->xsgdhp', part, eye8).reshape(nsg, L * 128, g8 * p)

    bin_m = jnp.concatenate([_bin(jnp.real(w_in)), _bin(jnp.imag(w_in))], axis=-1)

    v_out = cpw[1:L + 1].reshape(L, nsg, g8, c, p)

    def _cout(part):
        return jnp.einsum('txgcp,gh->xgpthc', part, eye8).reshape(nsg, g8 * p, L * 128)

    cout_m = jnp.concatenate([_cout(jnp.real(v_out)), _cout(-jnp.imag(v_out))], axis=1)

    nsteps = max(1, int(math.ceil(math.log2(n_chunks))))
    steps = (L * 2.0 ** jnp.arange(nsteps, dtype=F32))
    lp = jnp.exp(z[None] * steps[:, None, None].astype(jnp.complex64))
    lp = lp.reshape(nsteps, nsg, g8 * p)
    pows = jnp.stack([jnp.real(lp), jnp.imag(lp)], axis=1)
    pows = pows.transpose(2, 0, 1, 3).reshape(nsg, 2 * nsteps, g8 * p)
    return toep.astype(BF16), bin_m.astype(BF16), cout_m.astype(BF16), pows.astype(F32)


def _s5_kernel(u_ref, toep_ref, bin_ref, cout_ref, pw_ref, y_ref):
    L = S5_CHUNK
    nj = u_ref.shape[1]
    half = bin_ref.shape[-1] // 2
    uu = jnp.concatenate([u_ref[s] for s in range(L)], axis=1)
    e = _dot(uu, bin_ref[...])
    er, ei = e[:, :half], e[:, half:]
    row = lax.broadcasted_iota(jnp.int32, (nj, half), 0)

    def shifted(v, dd):
        return jnp.where(row >= dd, pltpu.roll(v, dd, 0), 0.0)

    nsteps = pw_ref.shape[0] // 2
    for k in range(nsteps):
        dd = 1 << k
        if dd >= nj:
            break
        lr = pw_ref[2 * k:2 * k + 1, :]
        li = pw_ref[2 * k + 1:2 * k + 2, :]
        sr, si = shifted(er, dd), shifted(ei, dd)
        er, ei = er + (lr * sr - li * si), ei + (lr * si + li * sr)
    xin = jnp.concatenate([shifted(er, 1), shifted(ei, 1)], axis=1).astype(BF16)
    y = _dot(uu, toep_ref[...]) + _dot(xin, cout_ref[...])
    for t in range(L):
        y_ref[t] = y[:, t * 128:(t + 1) * 128].astype(BF16)


def _s5(u2, prep, batch):
    toep, bin_m, cout_m, pows = prep
    t, tok = u2.shape
    L = S5_CHUNK
    nj = t // (batch * L)
    nsg = tok // V7X_LANES
    u4 = u2.reshape(batch, L, nj, tok)
    kk = L * V7X_LANES
    y4 = pl.pallas_call(
        _s5_kernel,
        grid=(nsg, batch),
        in_specs=[
            pl.BlockSpec((None, L, nj, V7X_LANES), lambda x, b: (b, 0, 0, x)),
            pl.BlockSpec((None, kk, kk), lambda x, b: (x, 0, 0)),
            pl.BlockSpec((None, kk, bin_m.shape[-1]), lambda x, b: (x, 0, 0)),
            pl.BlockSpec((None, cout_m.shape[1], kk), lambda x, b: (x, 0, 0)),
            pl.BlockSpec((None, pows.shape[1], pows.shape[2]), lambda x, b: (x, 0, 0)),
        ],
        out_specs=pl.BlockSpec((None, L, nj, V7X_LANES), lambda x, b: (b, 0, 0, x)),
        out_shape=jax.ShapeDtypeStruct((batch, L, nj, tok), BF16),
        compiler_params=_cparams(("parallel", "parallel")),
        name="s5",
    )(u4, toep, bin_m, cout_m, pows)
    return y4.reshape(t, tok)


def _mix_tail(x_ref, tok_bf, qm_ref, mkv_ref, wo_ref, g_ref, o_ref):
    tok = tok_bf.shape[-1]
    mem_o = _mem_attend(qm_ref[...], mkv_ref[...])
    o = _dot(tok_bf, wo_ref[:tok, :]) + _dot(mem_o, wo_ref[tok:, :])
    o_ref[...] = x_ref[...] + _rms(o, g_ref[...])


def _mix_out_a_kernel(x_ref, y_ref, qm_ref, mkv_ref, wglu_ref, bglu_ref, wo_ref, g_ref, o_ref):
    y = jax.nn.gelu(y_ref[...].astype(F32))
    gate = jax.nn.sigmoid(_dot(y.astype(BF16), wglu_ref[...]) + bglu_ref[...])
    _mix_tail(x_ref, (y * gate).astype(BF16), qm_ref, mkv_ref, wo_ref, g_ref, o_ref)


def _mix_out_b_kernel(x_ref, a_ref, qm_ref, mkv_ref, wo_ref, g_ref, o_ref):
    _mix_tail(x_ref, a_ref[...], qm_ref, mkv_ref, wo_ref, g_ref, o_ref)


def _mix_out(x2, tok_in, qm, mkv_all, norms, w_out, layer, seq, glu=None):
    t, d = x2.shape
    tok = tok_in.shape[-1]
    tm = _tile(t, 512)
    tiles_per_seq = seq // tm
    nm = mkv_all.shape[1] // (t // seq)
    row = lambda i: (i, 0)
    specs = [
        pl.BlockSpec((tm, d), row),
        pl.BlockSpec((tm, tok), row),
        pl.BlockSpec((tm, MEM_WIDTH), row),
        pl.BlockSpec((None, nm, 2 * MEM_WIDTH), lambda i: (layer, i // tiles_per_seq, 0)),
    ]
    args = [x2, tok_in, qm, mkv_all]
    if glu is not None:
        w_glu, b_glu, j = glu
        specs += [
            _const_spec((None, tok, tok), lambda i: (j, 0, 0)),
            _const_spec((None, 1, tok), lambda i: (j, 0, 0)),
        ]
        args += [w_glu, b_glu]
    specs += [
        _const_spec((None, d, d), lambda i: (layer, 0, 0)),
        _const_spec((None, None, 1, d), lambda i: (layer, 3, 0, 0)),
    ]
    args += [w_out, norms]
    return pl.pallas_call(
        _mix_out_a_kernel if glu is not None else _mix_out_b_kernel,
        grid=(t // tm,),
        in_specs=specs,
        out_specs=pl.BlockSpec((tm, d), row),
        out_shape=jax.ShapeDtypeStruct((t, d), F32),
        compiler_params=_cparams(("parallel",)),
        name="mix_out_a" if glu is not None else "mix_out_b",
    )(*args)


def _kv_kernel(x_ref, pos_ref, frq_ref, sgn_ref, gin_ref, wd_ref, gkv_ref, wuk_ref, wuv_ref, wkr_ref,
               k_ref, v_ref, rc_ref, rs_ref):
    ang = pos_ref[...].astype(F32) * frq_ref[...]
    rc = jnp.cos(ang)
    rs = jnp.sin(ang) * sgn_ref[...]
    rc_ref[...] = rc
    rs_ref[...] = rs
    h = _rms(x_ref[...], gin_ref[...]).astype(BF16)
    ckv = _rms(_dot(h, wd_ref[...]), gkv_ref[...]).astype(BF16)
    kn = _dot(ckv, wuk_ref[...])
    v_ref[...] = _dot(ckv, wuv_ref[...]).astype(BF16)
    kr2 = _dot(h, wkr_ref[...])
    kr = (kr2[:, :V7X_LANES] * rc + kr2[:, V7X_LANES:] * rs).astype(BF16)
    heads = k_ref.shape[-1] // Q_SLOT
    for hh in range(heads):
        k_ref[:, hh * Q_SLOT:hh * Q_SLOT + MLA_NOPE] = kn[:, hh * MLA_NOPE:(hh + 1) * MLA_NOPE].astype(BF16)
        k_ref[:, hh * Q_SLOT + MLA_NOPE:(hh + 1) * Q_SLOT] = kr


def _kv(x2, pos2, frq, sgn, g_in, w_dkv, g_kv, w_uk, w_uv, w_kr2):
    t, d = x2.shape
    r = w_dkv.shape[-1]
    hv = w_uv.shape[-1]
    heads = hv // MLA_V
    tm = _tile(t, 512)
    row = lambda i: (i, 0)
    c2 = lambda i: (0, 0)
    return pl.pallas_call(
        _kv_kernel,
        grid=(t // tm,),
        in_specs=[
            pl.BlockSpec((tm, d), row),
            pl.BlockSpec((tm, 1), row),
            _const_spec((1, V7X_LANES), c2),
            _const_spec((1, V7X_LANES), c2),
            _const_spec((1, d), c2),
            _const_spec((d, r), c2),
            _const_spec((1, r), c2),
            _const_spec((r, heads * MLA_NOPE), c2),
            _const_spec((r, hv), c2),
            _const_spec((d, 2 * V7X_LANES), c2),
        ],
        out_specs=[
            pl.BlockSpec((tm, heads * Q_SLOT), row),
            pl.BlockSpec((tm, hv), row),
            pl.BlockSpec((tm, V7X_LANES), row),
            pl.BlockSpec((tm, V7X_LANES), row),
        ],
        out_shape=[
            jax.ShapeDtypeStruct((t, heads * Q_SLOT), BF16),
            jax.ShapeDtypeStruct((t, hv), BF16),
            jax.ShapeDtypeStruct((t, V7X_LANES), F32),
            jax.ShapeDtypeStruct((t, V7X_LANES), F32),
        ],
        compiler_params=_cparams(("parallel",)),
        name="kv",
    )(x2, pos2, frq, sgn, g_in, w_dkv, g_kv, w_uk, w_uv, w_kr2)


def _b_in_kernel(x_ref, rc_ref, rs_ref, g_ref, w_ref, gq_ref, wq_ref, q_ref, qm_ref):
    h = _rms(x_ref[...], g_ref[...]).astype(BF16)
    z = _dot(h, w_ref[...])
    rank = gq_ref.shape[-1]
    qm_ref[...] = z[:, rank:].astype(BF16)
    cq = _rms(z[:, :rank], gq_ref[...]).astype(BF16)
    zq = _dot(cq, wq_ref[...])
    heads = q_ref.shape[-1] // Q_SLOT
    qs = (MLA_NOPE + MLA_ROPE) ** -0.5 * LOG2E
    rc, rs = rc_ref[...] * qs, rs_ref[...] * qs
    sw0 = heads * Q_SLOT
    for hh in range(heads):
        lo = hh * Q_SLOT
        q_ref[:, lo:lo + MLA_NOPE] = (zq[:, lo:lo + MLA_NOPE] * qs).astype(BF16)
        main = zq[:, lo + MLA_NOPE:lo + Q_SLOT]
        swp = zq[:, sw0 + hh * V7X_LANES:sw0 + (hh + 1) * V7X_LANES]
        q_ref[:, lo + MLA_NOPE:lo + Q_SLOT] = (main * rc + swp * rs).astype(BF16)


def _b_in(x2, rc, rs, norms, w_in, g_q, w_q, layer, j):
    t, d = x2.shape
    n = w_in.shape[-1]
    rank = g_q.shape[-1]
    nq = w_q.shape[-1]
    heads = nq // (Q_SLOT + V7X_LANES)
    tm = _tile(t, 512)
    row = lambda i: (i, 0)
    return pl.pallas_call(
        _b_in_kernel,
        grid=(t // tm,),
        in_specs=[
            pl.BlockSpec((tm, d), row),
            pl.BlockSpec((tm, V7X_LANES), row),
            pl.BlockSpec((tm, V7X_LANES), row),
            _const_spec((None, None, 1, d), lambda i: (layer, 2, 0, 0)),
            _const_spec((None, d, n), lambda i: (j, 0, 0)),
            _const_spec((None, 1, rank), lambda i: (j, 0, 0)),
            _const_spec((None, rank, nq), lambda i: (j, 0, 0)),
        ],
        out_specs=[
            pl.BlockSpec((tm, heads * Q_SLOT), row),
            pl.BlockSpec((tm, MEM_WIDTH), row),
        ],
        out_shape=[
            jax.ShapeDtypeStruct((t, heads * Q_SLOT), BF16),
            jax.ShapeDtypeStruct((t, MEM_WIDTH), BF16),
        ],
        compiler_params=_cparams(("parallel",)),
        name="b_in",
    )(x2, rc, rs, norms, w_in, g_q, w_q)


ATTN_BLOCK = 512


def _attn_kernel(q_ref, k_ref, v_ref, o_ref, vx_ref, s_ref, *, blk):
    kv = 2 * blk
    n_qblocks = q_ref.shape[0] // kv

    vx_ref[:, :MLA_V] = v_ref[...]
    lane = lax.broadcasted_iota(jnp.int32, (vx_ref.shape[0], vx_ref.shape[1] - MLA_V), 1)
    vx_ref[:, MLA_V:] = jnp.where(lane == 0, 1.0, 0.0).astype(BF16)

    def keys_of(j, n):
        return pl.ds(pl.multiple_of(j * kv, kv), n)

    def scores(i, chain, j):
        q_rows = pl.ds(i * kv + chain * blk, blk)
        s_ref[chain] = _dot_nt(q_ref[q_rows, :], k_ref[keys_of(j, kv), :])

    def absorb(chain, j, carry, n_keys, diag_offset=None):
        m, acc = carry
        s = s_ref[chain, :, :n_keys]
        if diag_offset is not None:
            rows = lax.broadcasted_iota(jnp.int32, s.shape, 0)
            cols = lax.broadcasted_iota(jnp.int32, s.shape, 1)
            s = jnp.where(cols <= rows + diag_offset, s, -jnp.inf)
        m_new = jnp.maximum(m, jnp.max(s, axis=-1, keepdims=True))
        p = jnp.exp2(s - m_new).astype(BF16)
        acc = jnp.exp2(m - m_new) * acc + _dot(p, vx_ref[keys_of(j, n_keys), :])
        return m_new, acc

    init = (jnp.full((blk, 1), -jnp.inf, F32), jnp.zeros((blk, vx_ref.shape[1]), F32))
    scores(0, 0, 0)
    for i in range(n_qblocks):
        def kv_step(j, carry, i=i):
            ca, cb = carry
            scores(i, 1, j)
            ca = absorb(0, j, ca, kv)
            scores(i, 0, j + 1)
            cb = absorb(1, j, cb, kv)
            return ca, cb

        ca, cb = lax.fori_loop(0, i, kv_step, (init, init))
        scores(i, 1, i)
        ca = absorb(0, i, ca, blk, diag_offset=0)
        if i + 1 < n_qblocks:
            scores(i + 1, 0, 0)
        cb = absorb(1, i, cb, kv, diag_offset=blk)
        for chain, (_, acc) in enumerate((ca, cb)):
            out_rows = pl.ds(i * kv + chain * blk, blk)
            o_ref[out_rows, :] = (acc[:, :MLA_V] / acc[:, MLA_V:MLA_V + 1]).astype(BF16)


def _attn(q, k_all, v, batch, seq):
    t = q.shape[0]
    heads = v.shape[-1] // MLA_V
    blk = _tile(seq // 2, ATTN_BLOCK)
    per_head = lambda b, h: (b, h)
    return pl.pallas_call(
        functools.partial(_attn_kernel, blk=blk),
        grid=(batch, heads),
        in_specs=[
            pl.BlockSpec((seq, Q_SLOT), per_head),
            pl.BlockSpec((seq, Q_SLOT), per_head),
            pl.BlockSpec((seq, MLA_V), per_head),
        ],
        out_specs=pl.BlockSpec((seq, MLA_V), per_head),
        out_shape=jax.ShapeDtypeStruct((t, heads * MLA_V), BF16),
        scratch_shapes=[pltpu.VMEM((seq, 2 * MLA_V), BF16), pltpu.VMEM((2, blk, 2 * blk), F32)],
        compiler_params=_cparams(("parallel", "parallel")),
        name="attn",
    )(q, k_all, v)


def _pad_cols(w, n):
    return jnp.pad(w, ((0, 0),) * (w.ndim - 1) + ((0, n - w.shape[-1]),))


def _q_weight(w_uq):
    nb, r, n = w_uq.shape
    heads = n // (MLA_NOPE + MLA_ROPE)
    w = w_uq.reshape(nb, r, heads, MLA_NOPE + MLA_ROPE)
    half = MLA_ROPE // 2
    nope, t1, t2 = w[..., :MLA_NOPE], w[..., MLA_NOPE:MLA_NOPE + half], w[..., MLA_NOPE + half:]
    main = _pad_cols(jnp.concatenate([nope, t1, t2], axis=-1), Q_SLOT).reshape(nb, r, heads * Q_SLOT)
    swp = _pad_cols(jnp.concatenate([t2, t1], axis=-1), V7X_LANES).reshape(nb, r, heads * V7X_LANES)
    return jnp.concatenate([main, swp], axis=-1)


def kernel(x, mem, positions, norms, ffn_w_gate, ffn_w_up, ffn_w_down, w_out, mem_norm, mem_w_kv, a_w_in, s5_lambda_re, s5_lambda_im, s5_b_re, s5_b_im, s5_c_re, s5_c_im, s5_d, s5_log_dt, s5_w_glu, s5_b_glu, b_w_in, mla_q_norm, mla_w_uq, kv_in_norm, w_dkv, kv_norm, w_uk, w_uv, w_kr):
    bsz, seq, d = x.shape
    t = bsz * seq
    depth = norms.shape[0]
    n_a = a_w_in.shape[0]
    n_b = b_w_in.shape[0]
    L = S5_CHUNK
    nj = seq // L
    half = MLA_ROPE // 2

    norms4 = norms.reshape(depth, norms.shape[1], 1, d)
    wg, wu, wd = ffn_w_gate.astype(BF16), ffn_w_up.astype(BF16), ffn_w_down.astype(BF16)
    w_out_b = w_out.astype(BF16)
    mkv_all = _mem_kv(mem.reshape(bsz * mem.shape[1], d), mem_norm.reshape(depth, 1, d),
                      mem_w_kv.astype(BF16))

    xp = x.reshape(bsz, nj, L, d).transpose(0, 2, 1, 3).reshape(t, d)
    a_w_in_b = a_w_in.astype(BF16)
    w_glu_b = s5_w_glu.astype(BF16)
    b_glu3 = s5_b_glu.reshape(n_a, 1, -1)
    for l in range(n_a):
        prep = _s5_prep(s5_lambda_re[l], s5_lambda_im[l], s5_b_re[l], s5_b_im[l], s5_c_re[l], s5_c_im[l],
                        s5_d[l], s5_log_dt[l], nj)
        xp = _ffn(xp, norms4, wg, wu, wd, l, 0)
        u, qm = _a_in(xp, norms4, a_w_in_b, l)
        y = _s5(u, prep, bsz)
        xp = _mix_out(xp, y, qm, mkv_all, norms4, w_out_b, l, seq, glu=(w_glu_b, b_glu3, l))
        xp = _ffn(xp, norms4, wg, wu, wd, l, 1)
    x2 = xp.reshape(bsz, L, nj, d).transpose(0, 2, 1, 3).reshape(t, d)

    inv_freq = ROPE_THETA ** (-jnp.arange(0, MLA_ROPE, 2, dtype=F32) / MLA_ROPE)
    zeros = jnp.zeros((V7X_LANES - MLA_ROPE,), F32)
    frq = jnp.concatenate([inv_freq, inv_freq, zeros]).reshape(1, V7X_LANES)
    sgn = jnp.concatenate([-jnp.ones((half,), F32), jnp.ones((half,), F32), zeros]).reshape(1, V7X_LANES)
    kr1, kr2 = w_kr[:, :half], w_kr[:, half:]
    w_kr2 = jnp.concatenate([_pad_cols(w_kr, V7X_LANES),
                             _pad_cols(jnp.concatenate([kr2, kr1], axis=-1), V7X_LANES)], axis=-1).astype(BF16)
    k_all, v, rc, rs = _kv(x2, positions.reshape(t, 1), frq, sgn, kv_in_norm.reshape(1, d),
                           w_dkv.astype(BF16), kv_norm.reshape(1, -1), w_uk.astype(BF16),
                           w_uv.astype(BF16), w_kr2)

    b_w_in_b = b_w_in.astype(BF16)
    w_q = _q_weight(mla_w_uq).astype(BF16)
    g_q = mla_q_norm.reshape(n_b, 1, -1)
    for j in range(n_b):
        l = n_a + j
        x2 = _ffn(x2, norms4, wg, wu, wd, l, 0)
        q, qm = _b_in(x2, rc, rs, norms4, b_w_in_b, g_q, w_q, l, j)
        o = _attn(q, k_all, v, bsz, seq)
        x2 = _mix_out(x2, o, qm, mkv_all, norms4, w_out_b, l, seq)
        x2 = _ffn(x2, norms4, wg, wu, wd, l, 1)
    return x2.reshape(bsz, seq, d)
```

```python
import functools
import math

import jax
import jax.numpy as jnp
from jax import lax
from jax.experimental import pallas as pl
from jax.experimental.pallas import tpu as pltpu

F32 = jnp.float32
BF16 = jnp.bfloat16

EPS = 1e-6
ROPE_THETA = 10000.0
LOG2E = 1.4426950408889634

V7X_LANES = 128
V7X_VMEM_BYTES = 64 * 1024 * 1024
VMEM_LIMIT = V7X_VMEM_BYTES - 6 * 1024 * 1024

S5_GROUP = 16
MLA_NOPE = 128
MLA_ROPE = 64
MLA_V = 128
MEM_HEADS = 4
MEM_HEAD_DIM = 128
MEM_WIDTH = MEM_HEADS * MEM_HEAD_DIM

S5_CHUNK = 8
S5_ROWS_PER_STEP = 2
S5_GROUPS_PER_TILE = V7X_LANES // S5_GROUP
Q_SLOT = 2 * V7X_LANES


def _tile(n, want):
    t = min(n, want)
    assert n % t == 0, (n, want)
    return t


def _cparams(sem):
    return pltpu.CompilerParams(dimension_semantics=sem, vmem_limit_bytes=VMEM_LIMIT)


def _const_spec(shape, index_map):
    return pl.BlockSpec(shape, index_map, pipeline_mode=pl.Buffered(1))


def _rms(xf, g):
    ms = jnp.mean(xf * xf, axis=-1, keepdims=True)
    return xf * lax.rsqrt(ms + EPS) * g


def _dot(a, b):
    return jnp.dot(a, b, preferred_element_type=F32)


def _dot_nt(a, b):
    return lax.dot_general(a, b, (((1,), (1,)), ((), ())), preferred_element_type=F32)


FFN_ROW_CHUNK = 256


def _ffn_kernel(x_ref, g0_ref, g1_ref, wg_ref, wu_ref, wd_ref, o_ref, h_ref, *, n_ff_steps):
    f = pl.program_id(1)
    rc = min(FFN_ROW_CHUNK, x_ref.shape[0])
    n_chunks = x_ref.shape[0] // rc

    def step(first, final):
        for r in range(n_chunks):
            rows = pl.ds(r * rc, rc)
            if first:
                h = _rms(x_ref[rows, :], g0_ref[...]).astype(BF16)
                h_ref[rows, :] = h
            else:
                h = h_ref[rows, :]
            g = _dot(h, wg_ref[...])
            u = _dot(h, wu_ref[...])
            a = (g * jax.nn.sigmoid(g) * u).astype(BF16)
            o = _dot(a, wd_ref[...])
            if not first:
                o = o_ref[rows, :] + o
            if final:
                o = x_ref[rows, :] + 0.5 * _rms(o, g1_ref[...])
            o_ref[rows, :] = o

    if n_ff_steps == 1:
        step(True, True)
        return
    pl.when(f == 0)(lambda: step(True, False))
    if n_ff_steps > 2:
        pl.when(jnp.logical_and(f > 0, f < n_ff_steps - 1))(lambda: step(False, False))
    pl.when(f == n_ff_steps - 1)(lambda: step(False, True))


def _ffn(x2, norms, wg, wu, wd, layer, half):
    t, d = x2.shape
    ff = wg.shape[-1]
    tm = _tile(t, 1024)
    tf = _tile(ff, 512)
    n0, n1 = (0, 1) if half == 0 else (4, 5)
    return pl.pallas_call(
        functools.partial(_ffn_kernel, n_ff_steps=ff // tf),
        grid=(t // tm, ff // tf),
        in_specs=[
            pl.BlockSpec((tm, d), lambda i, f: (i, 0)),
            pl.BlockSpec((None, None, 1, d), lambda i, f: (layer, n0, 0, 0)),
            pl.BlockSpec((None, None, 1, d), lambda i, f: (layer, n1, 0, 0)),
            pl.BlockSpec((None, None, d, tf), lambda i, f: (layer, half, 0, f)),
            pl.BlockSpec((None, None, d, tf), lambda i, f: (layer, half, 0, f)),
            pl.BlockSpec((None, None, tf, d), lambda i, f: (layer, half, f, 0)),
        ],
        out_specs=pl.BlockSpec((tm, d), lambda i, f: (i, 0)),
        out_shape=jax.ShapeDtypeStruct((t, d), F32),
        scratch_shapes=[pltpu.VMEM((tm, d), BF16)],
        compiler_params=_cparams(("parallel", "arbitrary")),
        name="ffn",
    )(x2, norms, norms, wg, wu, wd)


def _mem_kv_kernel(mem_ref, g_ref, w_ref, o_ref):
    h = _rms(mem_ref[...], g_ref[...]).astype(BF16)
    o_ref[...] = _dot(h, w_ref[...]).astype(BF16)


def _mem_kv(mem2, mem_norm, w_kv):
    depth, d, n = w_kv.shape
    rows = mem2.shape[0]
    return pl.pallas_call(
        _mem_kv_kernel,
        grid=(depth,),
        in_specs=[
            pl.BlockSpec((rows, d), lambda l: (0, 0)),
            pl.BlockSpec((None, 1, d), lambda l: (l, 0, 0)),
            pl.BlockSpec((None, d, n), lambda l: (l, 0, 0)),
        ],
        out_specs=pl.BlockSpec((None, rows, n), lambda l: (l, 0, 0)),
        out_shape=jax.ShapeDtypeStruct((depth, rows, n), BF16),
        compiler_params=_cparams(("parallel",)),
        name="mem_kv",
    )(mem2, mem_norm, w_kv)


def _mem_attend(qm, mkv):
    scale = MEM_HEAD_DIM ** -0.5
    outs = []
    for hd in range(MEM_HEADS):
        lo = hd * MEM_HEAD_DIM
        qh = qm[:, lo:lo + MEM_HEAD_DIM]
        kh = mkv[:, lo:lo + MEM_HEAD_DIM]
        vh = mkv[:, MEM_WIDTH + lo:MEM_WIDTH + lo + MEM_HEAD_DIM]
        s = _dot_nt(qh, kh) * scale
        e = jnp.exp(s - jnp.max(s, axis=-1, keepdims=True))
        p = e / jnp.sum(e, axis=-1, keepdims=True)
        outs.append(_dot(p.astype(BF16), vh).astype(BF16))
    return jnp.concatenate(outs, axis=-1)


def _a_in_kernel(x_ref, g_ref, w_ref, u_ref, qm_ref):
    h = _rms(x_ref[...], g_ref[...]).astype(BF16)
    z = _dot(h, w_ref[...])
    tok = u_ref.shape[-1]
    u_ref[...] = z[:, :tok].astype(BF16)
    qm_ref[...] = z[:, tok:].astype(BF16)


def _a_in(x2, norms, w_in, layer):
    t, d = x2.shape
    n = w_in.shape[-1]
    tok = n - MEM_WIDTH
    tm = _tile(t, 512)
    return pl.pallas_call(
        _a_in_kernel,
        grid=(t // tm,),
        in_specs=[
            pl.BlockSpec((tm, d), lambda i: (i, 0)),
            _const_spec((None, None, 1, d), lambda i: (layer, 2, 0, 0)),
            _const_spec((None, d, n), lambda i: (layer, 0, 0)),
        ],
        out_specs=[
            pl.BlockSpec((tm, tok), lambda i: (i, 0)),
            pl.BlockSpec((tm, MEM_WIDTH), lambda i: (i, 0)),
        ],
        out_shape=[
            jax.ShapeDtypeStruct((t, tok), BF16),
            jax.ShapeDtypeStruct((t, MEM_WIDTH), BF16),
        ],
        compiler_params=_cparams(("parallel",)),
        name="a_in",
    )(x2, norms, w_in)


def _s5_prep(lam_re, lam_im, b_re, b_im, c_re, c_im, d, log_dt, n_chunks):
    L = S5_CHUNK
    g, p = lam_re.shape
    c = b_re.shape[-1]
    g8 = S5_GROUPS_PER_TILE
    nsg = g // g8
    dt = jnp.exp(log_dt)[:, None]
    zr, zi = lam_re * dt, lam_im * dt

    def powers(mult):
        m = mult[:, None, None]
        mag = jnp.exp(m * zr[None])
        return ((mag * jnp.cos(m * zi[None])).reshape(-1, nsg, g8 * p),
                (mag * jnp.sin(m * zi[None])).reshape(-1, nsg, g8 * p))

    nsteps = max(1, int(math.ceil(math.log2(n_chunks))))
    mult = jnp.concatenate([jnp.arange(L + 1, dtype=F32), L * 2.0 ** jnp.arange(nsteps, dtype=F32)])
    pr, pi = powers(mult)
    tab = jnp.stack([pr, pi], axis=1).reshape(2 * mult.shape[0], nsg, g8 * p).transpose(1, 0, 2)

    er, ei = jnp.exp(zr) * jnp.cos(zi) - 1.0, jnp.exp(zr) * jnp.sin(zi)
    den = lam_re * lam_re + lam_im * lam_im
    fr, fi = (er * lam_re + ei * lam_im) / den, (ei * lam_re - er * lam_im) / den
    bb_r = fr[..., None] * b_re - fi[..., None] * b_im
    bb_i = fr[..., None] * b_im + fi[..., None] * b_re

    eye8 = jnp.eye(g8, dtype=F32)

    def expand(x):
        x = x.reshape(nsg, g8, c, p).transpose(0, 2, 1, 3)
        x = x[:, None] * eye8[None, :, None, :, None]
        return x.reshape(nsg, g8 * c, g8 * p)

    bt = (expand(jnp.swapaxes(bb_r, 1, 2)), expand(jnp.swapaxes(bb_i, 1, 2)))
    ct = (expand(c_re), expand(c_im))
    return bt + ct + (tab, d.reshape(nsg, 1, g8 * c))


def _s5_kernel(u_ref, btr_ref, bti_ref, ctr_ref, cti_ref, tab_ref, d_ref, y_ref,
               toep_ref, bin_ref, coutt_ref):
    L = S5_CHUNK
    nj = u_ref.shape[2]
    lanes = V7X_LANES
    half = bin_ref.shape[-1] // 2

    @pl.when(pl.program_id(1) == 0)
    def _():
        btr, bti, ctr, cti = btr_ref[...], bti_ref[...], ctr_ref[...], cti_ref[...]
        ccat = jnp.concatenate([ctr, cti], axis=1)
        toep_ref[...] = jnp.zeros(toep_ref.shape, BF16)
        for tau in range(L + 1):
            lr = tab_ref[2 * tau:2 * tau + 1, :]
            li = tab_ref[2 * tau + 1:2 * tau + 2, :]
            if tau < L:
                yr, yi = btr * lr - bti * li, btr * li + bti * lr
                s = L - 1 - tau
                bin_ref[s * lanes:(s + 1) * lanes, :half] = yr.astype(BF16)
                bin_ref[s * lanes:(s + 1) * lanes, half:] = yi.astype(BF16)
                bd = lax.dot_general(jnp.concatenate([yr, -yi], axis=1), ccat, (((1,), (1,)), ((), ())),
                                     precision=lax.Precision.HIGHEST, preferred_element_type=F32)
                if tau == 0:
                    r = lax.broadcasted_iota(jnp.int32, bd.shape, 0)
                    cidx = lax.broadcasted_iota(jnp.int32, bd.shape, 1)
                    bd = bd + jnp.where(r == cidx, d_ref[...], 0.0)
                bd = bd.astype(BF16)
                for s in range(L - tau):
                    t = s + tau
                    toep_ref[s * lanes:(s + 1) * lanes, t * lanes:(t + 1) * lanes] = bd
            if tau >= 1:
                t = tau - 1
                coutt_ref[t * lanes:(t + 1) * lanes, :half] = (ctr * lr - cti * li).astype(BF16)
                coutt_ref[t * lanes:(t + 1) * lanes, half:] = (-(ctr * li + cti * lr)).astype(BF16)

    row = lax.broadcasted_iota(jnp.int32, (nj, half), 0)

    def shifted(v, dd):
        if dd % 8 == 0:
            return jnp.concatenate([jnp.zeros((dd, v.shape[1]), v.dtype), v[:nj - dd]], axis=0)
        return jnp.where(row >= dd, pltpu.roll(v, dd, 0), 0.0)

    k0 = 2 * (L + 1)
    nsteps = (tab_ref.shape[0] - k0) // 2
    for b in range(u_ref.shape[0]):
        uu = jnp.concatenate([u_ref[b, s] for s in range(L)], axis=1)
        e = _dot(uu, bin_ref[...])
        er, ei = e[:, :half], e[:, half:]
        for k in range(nsteps):
            dd = 1 << k
            if dd >= nj:
                break
            lr = tab_ref[k0 + 2 * k:k0 + 2 * k + 1, :]
            li = tab_ref[k0 + 2 * k + 1:k0 + 2 * k + 2, :]
            sr, si = shifted(er, dd), shifted(ei, dd)
            er, ei = er + (lr * sr - li * si), ei + (lr * si + li * sr)
        xin = jnp.concatenate([shifted(er, 1), shifted(ei, 1)], axis=1).astype(BF16)
        y = _dot(uu, toep_ref[...]) + _dot_nt(xin, coutt_ref[...])
        for t in range(L):
            y_ref[b, t] = y[:, t * lanes:(t + 1) * lanes].astype(BF16)


def _s5(u2, prep, batch):
    btr, bti, ctr, cti, tab, dd = prep
    t, tok = u2.shape
    L = S5_CHUNK
    nj = t // (batch * L)
    nsg = tok // V7X_LANES
    u4 = u2.reshape(batch, L, nj, tok)
    kk = L * V7X_LANES
    ns = btr.shape[-1]
    per_tile = lambda x, b: (x, 0, 0)
    rows = _tile(batch, S5_ROWS_PER_STEP)
    y4 = pl.pallas_call(
        _s5_kernel,
        grid=(nsg, batch // rows),
        in_specs=[pl.BlockSpec((rows, L, nj, V7X_LANES), lambda x, b: (b, 0, 0, x))]
        + [pl.BlockSpec((None, V7X_LANES, ns), per_tile)] * 4
        + [pl.BlockSpec((None, tab.shape[1], ns), per_tile), pl.BlockSpec((None, 1, V7X_LANES), per_tile)],
        out_specs=pl.BlockSpec((rows, L, nj, V7X_LANES), lambda x, b: (b, 0, 0, x)),
        out_shape=jax.ShapeDtypeStruct((batch, L, nj, tok), BF16),
        scratch_shapes=[pltpu.VMEM((kk, kk), BF16), pltpu.VMEM((kk, 2 * ns), BF16),
                        pltpu.VMEM((kk, 2 * ns), BF16)],
        compiler_params=_cparams(("parallel", "arbitrary")),
        name="s5",
    )(u4, btr, bti, ctr, cti, tab, dd)
    return y4.reshape(t, tok)


def _mix_tail(x_ref, tok_bf, qm_ref, mkv_ref, wo_ref, g_ref, o_ref):
    tok = tok_bf.shape[-1]
    mem_o = _mem_attend(qm_ref[...], mkv_ref[...])
    o = _dot(tok_bf, wo_ref[:tok, :]) + _dot(mem_o, wo_ref[tok:, :])
    o_ref[...] = x_ref[...] + _rms(o, g_ref[...])


def _mix_out_a_kernel(x_ref, y_ref, qm_ref, mkv_ref, wglu_ref, bglu_ref, wo_ref, g_ref, o_ref):
    y = jax.nn.gelu(y_ref[...].astype(F32))
    gate = jax.nn.sigmoid(_dot(y.astype(BF16), wglu_ref[...]) + bglu_ref[...])
    _mix_tail(x_ref, (y * gate).astype(BF16), qm_ref, mkv_ref, wo_ref, g_ref, o_ref)


def _mix_out_b_kernel(x_ref, a_ref, qm_ref, mkv_ref, wo_ref, g_ref, o_ref):
    _mix_tail(x_ref, a_ref[...], qm_ref, mkv_ref, wo_ref, g_ref, o_ref)


def _mix_out(x2, tok_in, qm, mkv_all, norms, w_out, layer, seq, glu=None):
    t, d = x2.shape
    tok = tok_in.shape[-1]
    tm = _tile(t, 512)
    tiles_per_seq = seq // tm
    nm = mkv_all.shape[1] // (t // seq)
    row = lambda i: (i, 0)
    specs = [
        pl.BlockSpec((tm, d), row),
        pl.BlockSpec((tm, tok), row),
        pl.BlockSpec((tm, MEM_WIDTH), row),
        pl.BlockSpec((None, nm, 2 * MEM_WIDTH), lambda i: (layer, i // tiles_per_seq, 0)),
    ]
    args = [x2, tok_in, qm, mkv_all]
    if glu is not None:
        w_glu, b_glu, j = glu
        specs += [
            _const_spec((None, tok, tok), lambda i: (j, 0, 0)),
            _const_spec((None, 1, tok), lambda i: (j, 0, 0)),
        ]
        args += [w_glu, b_glu]
    specs += [
        _const_spec((None, d, d), lambda i: (layer, 0, 0)),
        _const_spec((None, None, 1, d), lambda i: (layer, 3, 0, 0)),
    ]
    args += [w_out, norms]
    return pl.pallas_call(
        _mix_out_a_kernel if glu is not None else _mix_out_b_kernel,
        grid=(t // tm,),
        in_specs=specs,
        out_specs=pl.BlockSpec((tm, d), row),
        out_shape=jax.ShapeDtypeStruct((t, d), F32),
        compiler_params=_cparams(("parallel",)),
        name="mix_out_a" if glu is not None else "mix_out_b",
    )(*args)


def _kv_kernel(x_ref, pos_ref, frq_ref, sgn_ref, gin_ref, wd_ref, gkv_ref, wuk_ref, wuv_ref, wkr_ref,
               k_ref, v_ref, rc_ref, rs_ref):
    ang = pos_ref[...].astype(F32) * frq_ref[...]
    rc = jnp.cos(ang)
    rs = jnp.sin(ang) * sgn_ref[...]
    rc_ref[...] = rc
    rs_ref[...] = rs
    h = _rms(x_ref[...], gin_ref[...]).astype(BF16)
    ckv = _rms(_dot(h, wd_ref[...]), gkv_ref[...]).astype(BF16)
    kn = _dot(ckv, wuk_ref[...])
    v_ref[...] = _dot(ckv, wuv_ref[...]).astype(BF16)
    kr2 = _dot(h, wkr_ref[...])
    kr = (kr2[:, :V7X_LANES] * rc + kr2[:, V7X_LANES:] * rs).astype(BF16)
    heads = k_ref.shape[-1] // Q_SLOT
    for hh in range(heads):
        k_ref[:, hh * Q_SLOT:hh * Q_SLOT + MLA_NOPE] = kn[:, hh * MLA_NOPE:(hh + 1) * MLA_NOPE].astype(BF16)
        k_ref[:, hh * Q_SLOT + MLA_NOPE:(hh + 1) * Q_SLOT] = kr


def _kv(x2, pos2, frq, sgn, g_in, w_dkv, g_kv, w_uk, w_uv, w_kr2):
    t, d = x2.shape
    r = w_dkv.shape[-1]
    hv = w_uv.shape[-1]
    heads = hv // MLA_V
    tm = _tile(t, 512)
    row = lambda i: (i, 0)
    c2 = lambda i: (0, 0)
    return pl.pallas_call(
        _kv_kernel,
        grid=(t // tm,),
        in_specs=[
            pl.BlockSpec((tm, d), row),
            pl.BlockSpec((tm, 1), row),
            _const_spec((1, V7X_LANES), c2),
            _const_spec((1, V7X_LANES), c2),
            _const_spec((1, d), c2),
            _const_spec((d, r), c2),
            _const_spec((1, r), c2),
            _const_spec((r, heads * MLA_NOPE), c2),
            _const_spec((r, hv), c2),
            _const_spec((d, 2 * V7X_LANES), c2),
        ],
        out_specs=[
            pl.BlockSpec((tm, heads * Q_SLOT), row),
            pl.BlockSpec((tm, hv), row),
            pl.BlockSpec((tm, V7X_LANES), row),
            pl.BlockSpec((tm, V7X_LANES), row),
        ],
        out_shape=[
            jax.ShapeDtypeStruct((t, heads * Q_SLOT), BF16),
            jax.ShapeDtypeStruct((t, hv), BF16),
            jax.ShapeDtypeStruct((t, V7X_LANES), F32),
            jax.ShapeDtypeStruct((t, V7X_LANES), F32),
        ],
        compiler_params=_cparams(("parallel",)),
        name="kv",
    )(x2, pos2, frq, sgn, g_in, w_dkv, g_kv, w_uk, w_uv, w_kr2)


def _b_in_kernel(x_ref, rc_ref, rs_ref, g_ref, w_ref, gq_ref, wq_ref, q_ref, qm_ref):
    h = _rms(x_ref[...], g_ref[...]).astype(BF16)
    z = _dot(h, w_ref[...])
    rank = gq_ref.shape[-1]
    qm_ref[...] = z[:, rank:].astype(BF16)
    cq = _rms(z[:, :rank], gq_ref[...]).astype(BF16)
    zq = _dot(cq, wq_ref[...])
    heads = q_ref.shape[-1] // Q_SLOT
    qs = (MLA_NOPE + MLA_ROPE) ** -0.5 * LOG2E
    rc, rs = rc_ref[...] * qs, rs_ref[...] * qs
    sw0 = heads * Q_SLOT
    for hh in range(heads):
        lo = hh * Q_SLOT
        q_ref[:, lo:lo + MLA_NOPE] = (zq[:, lo:lo + MLA_NOPE] * qs).astype(BF16)
        main = zq[:, lo + MLA_NOPE:lo + Q_SLOT]
        swp = zq[:, sw0 + hh * V7X_LANES:sw0 + (hh + 1) * V7X_LANES]
        q_ref[:, lo + MLA_NOPE:lo + Q_SLOT] = (main * rc + swp * rs).astype(BF16)


def _b_in(x2, rc, rs, norms, w_in, g_q, w_q, layer, j):
    t, d = x2.shape
    n = w_in.shape[-1]
    rank = g_q.shape[-1]
    nq = w_q.shape[-1]
    heads = nq // (Q_SLOT + V7X_LANES)
    tm = _tile(t, 512)
    row = lambda i: (i, 0)
    return pl.pallas_call(
        _b_in_kernel,
        grid=(t // tm,),
        in_specs=[
            pl.BlockSpec((tm, d), row),
            pl.BlockSpec((tm, V7X_LANES), row),
            pl.BlockSpec((tm, V7X_LANES), row),
            _const_spec((None, None, 1, d), lambda i: (layer, 2, 0, 0)),
            _const_spec((None, d, n), lambda i: (j, 0, 0)),
            _const_spec((None, 1, rank), lambda i: (j, 0, 0)),
            _const_spec((None, rank, nq), lambda i: (j, 0, 0)),
        ],
        out_specs=[
            pl.BlockSpec((tm, heads * Q_SLOT), row),
            pl.BlockSpec((tm, MEM_WIDTH), row),
        ],
        out_shape=[
            jax.ShapeDtypeStruct((t, heads * Q_SLOT), BF16),
            jax.ShapeDtypeStruct((t, MEM_WIDTH), BF16),
        ],
        compiler_params=_cparams(("parallel",)),
        name="b_in",
    )(x2, rc, rs, norms, w_in, g_q, w_q)


ATTN_BLOCK = 512


def _attn_kernel(q_ref, k_ref, v_ref, o_ref, vx_ref, s_ref, *, blk):
    kv = 2 * blk
    n_qblocks = q_ref.shape[0] // kv

    vx_ref[:, :MLA_V] = v_ref[...]
    lane = lax.broadcasted_iota(jnp.int32, (vx_ref.shape[0], vx_ref.shape[1] - MLA_V), 1)
    vx_ref[:, MLA_V:] = jnp.where(lane == 0, 1.0, 0.0).astype(BF16)

    def keys_of(j, n):
        return pl.ds(j * kv, n)

    def scores(i, chain, j):
        n_keys = blk if (chain == 0 and j == i) else kv
        q_rows = pl.ds(i * kv + chain * blk, blk)
        s_ref[chain, :, :n_keys] = _dot_nt(q_ref[q_rows, :], k_ref[keys_of(j, n_keys), :])

    def absorb(chain, j, carry, n_keys, diag_offset=None):
        m, acc = carry
        s = s_ref[chain, :, :n_keys]
        if diag_offset is not None:
            rows = lax.broadcasted_iota(jnp.int32, s.shape, 0)
            cols = lax.broadcasted_iota(jnp.int32, s.shape, 1)
            s = jnp.where(cols <= rows + diag_offset, s, -jnp.inf)
        m_new = jnp.maximum(m, jnp.max(s, axis=-1, keepdims=True))
        p = jnp.exp2(s - m_new).astype(BF16)
        acc = jnp.exp2(m - m_new) * acc + _dot(p, vx_ref[keys_of(j, n_keys), :])
        return m_new, acc

    init = (jnp.full((blk, 1), -jnp.inf, F32), jnp.zeros((blk, vx_ref.shape[1]), F32))
    scores(0, 0, 0)
    for i in range(n_qblocks):
        ca, cb = init, init
        for j in range(i):
            scores(i, 1, j)
            ca = absorb(0, j, ca, kv)
            scores(i, 0, j + 1)
            cb = absorb(1, j, cb, kv)
        scores(i, 1, i)
        ca = absorb(0, i, ca, blk, diag_offset=0)
        if i + 1 < n_qblocks:
            scores(i + 1, 0, 0)
        cb = absorb(1, i, cb, kv, diag_offset=blk)
        for chain, (_, acc) in enumerate((ca, cb)):
            out_rows = pl.ds(i * kv + chain * blk, blk)
            o_ref[out_rows, :] = (acc[:, :MLA_V] / acc[:, MLA_V:MLA_V + 1]).astype(BF16)


def _attn(q, k_all, v, batch, seq):
    t = q.shape[0]
    heads = v.shape[-1] // MLA_V
    blk = _tile(seq // 2, ATTN_BLOCK)
    per_head = lambda b, h: (b, h)
    return pl.pallas_call(
        functools.partial(_attn_kernel, blk=blk),
        grid=(batch, heads),
        in_specs=[
            pl.BlockSpec((seq, Q_SLOT), per_head),
            pl.BlockSpec((seq, Q_SLOT), per_head),
            pl.BlockSpec((seq, MLA_V), per_head),
        ],
        out_specs=pl.BlockSpec((seq, MLA_V), per_head),
        out_shape=jax.ShapeDtypeStruct((t, heads * MLA_V), BF16),
        scratch_shapes=[pltpu.VMEM((seq, 2 * MLA_V), BF16), pltpu.VMEM((2, blk, 2 * blk), F32)],
        compiler_params=_cparams(("parallel", "parallel")),
        name="attn",
    )(q, k_all, v)


def _pad_cols(w, n):
    return jnp.pad(w, ((0, 0),) * (w.ndim - 1) + ((0, n - w.shape[-1]),))


def _q_weight(w_uq):
    nb, r, n = w_uq.shape
    heads = n // (MLA_NOPE + MLA_ROPE)
    w = w_uq.reshape(nb, r, heads, MLA_NOPE + MLA_ROPE)
    half = MLA_ROPE // 2
    nope, t1, t2 = w[..., :MLA_NOPE], w[..., MLA_NOPE:MLA_NOPE + half], w[..., MLA_NOPE + half:]
    main = _pad_cols(jnp.concatenate([nope, t1, t2], axis=-1), Q_SLOT).reshape(nb, r, heads * Q_SLOT)
    swp = _pad_cols(jnp.concatenate([t2, t1], axis=-1), V7X_LANES).reshape(nb, r, heads * V7X_LANES)
    return jnp.concatenate([main, swp], axis=-1)


def kernel(x, mem, positions, norms, ffn_w_gate, ffn_w_up, ffn_w_down, w_out, mem_norm, mem_w_kv, a_w_in, s5_lambda_re, s5_lambda_im, s5_b_re, s5_b_im, s5_c_re, s5_c_im, s5_d, s5_log_dt, s5_w_glu, s5_b_glu, b_w_in, mla_q_norm, mla_w_uq, kv_in_norm, w_dkv, kv_norm, w_uk, w_uv, w_kr):
    bsz, seq, d = x.shape
    t = bsz * seq
    depth = norms.shape[0]
    n_a = a_w_in.shape[0]
    n_b = b_w_in.shape[0]
    L = S5_CHUNK
    nj = seq // L
    half = MLA_ROPE // 2

    norms4 = norms.reshape(depth, norms.shape[1], 1, d)
    wg, wu, wd = ffn_w_gate.astype(BF16), ffn_w_up.astype(BF16), ffn_w_down.astype(BF16)
    w_out_b = w_out.astype(BF16)
    mkv_all = _mem_kv(mem.reshape(bsz * mem.shape[1], d), mem_norm.reshape(depth, 1, d),
                      mem_w_kv.astype(BF16))

    xp = x.reshape(bsz, nj, L, d).transpose(0, 2, 1, 3).reshape(t, d)
    a_w_in_b = a_w_in.astype(BF16)
    w_glu_b = s5_w_glu.astype(BF16)
    b_glu3 = s5_b_glu.reshape(n_a, 1, -1)
    for l in range(n_a):
        prep = _s5_prep(s5_lambda_re[l], s5_lambda_im[l], s5_b_re[l], s5_b_im[l], s5_c_re[l], s5_c_im[l],
                        s5_d[l], s5_log_dt[l], nj)
        xp = _ffn(xp, norms4, wg, wu, wd, l, 0)
        u, qm = _a_in(xp, norms4, a_w_in_b, l)
        y = _s5(u, prep, bsz)
        xp = _mix_out(xp, y, qm, mkv_all, norms4, w_out_b, l, seq, glu=(w_glu_b, b_glu3, l))
        xp = _ffn(xp, norms4, wg, wu, wd, l, 1)
    x2 = xp.reshape(bsz, L, nj, d).transpose(0, 2, 1, 3).reshape(t, d)

    inv_freq = ROPE_THETA ** (-jnp.arange(0, MLA_ROPE, 2, dtype=F32) / MLA_ROPE)
    zeros = jnp.zeros((V7X_LANES - MLA_ROPE,), F32)
    frq = jnp.concatenate([inv_freq, inv_freq, zeros]).reshape(1, V7X_LANES)
    sgn = jnp.concatenate([-jnp.ones((half,), F32), jnp.ones((half,), F32), zeros]).reshape(1, V7X_LANES)
    kr1, kr2 = w_kr[:, :half], w_kr[:, half:]
    w_kr2 = jnp.concatenate([_pad_cols(w_kr, V7X_LANES),
                             _pad_cols(jnp.concatenate([kr2, kr1], axis=-1), V7X_LANES)], axis=-1).astype(BF16)
    k_all, v, rc, rs = _kv(x2, positions.reshape(t, 1), frq, sgn, kv_in_norm.reshape(1, d),
                           w_dkv.astype(BF16), kv_norm.reshape(1, -1), w_uk.astype(BF16),
                           w_uv.astype(BF16), w_kr2)

    b_w_in_b = b_w_in.astype(BF16)
    w_q = _q_weight(mla_w_uq).astype(BF16)
    g_q = mla_q_norm.reshape(n_b, 1, -1)
    for j in range(n_b):
        l = n_a + j
        x2 = _ffn(x2, norms4, wg, wu, wd, l, 0)
        q, qm = _b_in(x2, rc, rs, norms4, b_w_in_b, g_q, w_q, l, j)
        o = _attn(q, k_all, v, bsz, seq)
        x2 = _mix_out(x2, o, qm, mkv_all, norms4, w_out_b, l, seq)
        x2 = _ffn(x2, norms4, wg, wu, wd, l, 1)
    return x2.reshape(bsz, seq, d)
```

```python
import functools
import math

import jax
import jax.numpy as jnp
from jax import lax
from jax.experimental import pallas as pl
from jax.experimental.pallas import tpu as pltpu

F32 = jnp.float32
BF16 = jnp.bfloat16

EPS = 1e-6
ROPE_THETA = 10000.0
LOG2E = 1.4426950408889634

V7X_LANES = 128
V7X_VMEM_BYTES = 64 * 1024 * 1024
VMEM_LIMIT = V7X_VMEM_BYTES - 6 * 1024 * 1024

S5_GROUP = 16
MLA_NOPE = 128
MLA_ROPE = 64
MLA_V = 128
MEM_HEADS = 4
MEM_HEAD_DIM = 128
MEM_WIDTH = MEM_HEADS * MEM_HEAD_DIM

S5_CHUNK = 8
S5_ROWS_PER_STEP = 2
S5_GROUPS_PER_TILE = V7X_LANES // S5_GROUP
Q_SLOT = 2 * V7X_LANES

FFN_TOKEN_TILE = 1024
FFN_FF_TILE = 512
MIX_TOKEN_TILE = 512


def _tile(n, want):
    t = min(n, want)
    assert n % t == 0, (n, want)
    return t


def _cparams(sem):
    return pltpu.CompilerParams(dimension_semantics=sem, vmem_limit_bytes=VMEM_LIMIT)


def _const_spec(shape, index_map):
    return pl.BlockSpec(shape, index_map, pipeline_mode=pl.Buffered(1))


def _rms(xf, g):
    ms = jnp.mean(xf * xf, axis=-1, keepdims=True)
    return xf * lax.rsqrt(ms + EPS) * g


def _dot(a, b):
    return jnp.dot(a, b, preferred_element_type=F32)


def _dot_nt(a, b):
    return lax.dot_general(a, b, (((1,), (1,)), ((), ())), preferred_element_type=F32)


FFN_ROW_CHUNK = 256


def _ffn_kernel(*refs, n_ff_steps, convert_next):
    if convert_next:
        (x_ref, g0_ref, g1_ref, wg_ref, wu_ref, wd_ref, ng_ref, nu_ref, nd_ref,
         o_ref, cg_ref, cu_ref, cd_ref, h_ref) = refs
        cg_ref[...] = ng_ref[...].astype(BF16)
        cu_ref[...] = nu_ref[...].astype(BF16)
        cd_ref[...] = nd_ref[...].astype(BF16)
    else:
        x_ref, g0_ref, g1_ref, wg_ref, wu_ref, wd_ref, o_ref, h_ref = refs
    f = pl.program_id(1)
    rc = min(FFN_ROW_CHUNK, x_ref.shape[0])
    n_chunks = x_ref.shape[0] // rc

    def step(first, final):
        for r in range(n_chunks):
            rows = pl.ds(r * rc, rc)
            if first:
                h = _rms(x_ref[rows, :], g0_ref[...]).astype(BF16)
                h_ref[rows, :] = h
            else:
                h = h_ref[rows, :]
            g = _dot(h, wg_ref[...])
            u = _dot(h, wu_ref[...])
            a = (g * jax.nn.sigmoid(g) * u).astype(BF16)
            o = _dot(a, wd_ref[...])
            if not first:
                o = o_ref[rows, :] + o
            if final:
                o = x_ref[rows, :] + 0.5 * _rms(o, g1_ref[...])
            o_ref[rows, :] = o

    if n_ff_steps == 1:
        step(True, True)
        return
    pl.when(f == 0)(lambda: step(True, False))
    if n_ff_steps > 2:
        pl.when(jnp.logical_and(f > 0, f < n_ff_steps - 1))(lambda: step(False, False))
    pl.when(f == n_ff_steps - 1)(lambda: step(False, True))


def _ffn(x2, norms, weights, layer, half, next_f32=None):
    wg, wu, wd = weights
    t, d = x2.shape
    ff = wg.shape[-1]
    tm = _tile(t, FFN_TOKEN_TILE)
    tf = _tile(ff, FFN_FF_TILE)
    n_row_tiles = t // tm
    n0, n1 = (0, 1) if half == 0 else (4, 5)
    in_specs = [
        pl.BlockSpec((tm, d), lambda i, f: (i, 0)),
        pl.BlockSpec((None, None, 1, d), lambda i, f: (layer, n0, 0, 0)),
        pl.BlockSpec((None, None, 1, d), lambda i, f: (layer, n1, 0, 0)),
        pl.BlockSpec((d, tf), lambda i, f: (0, f)),
        pl.BlockSpec((d, tf), lambda i, f: (0, f)),
        pl.BlockSpec((tf, d), lambda i, f: (f, 0)),
    ]
    out_specs = [pl.BlockSpec((tm, d), lambda i, f: (i, 0))]
    out_shape = [jax.ShapeDtypeStruct((t, d), F32)]
    args = [x2, norms, norms, wg, wu, wd]
    if next_f32 is not None:
        ng, nu, nd, ln, hn = next_f32
        dr = d // n_row_tiles
        in_specs += [
            pl.BlockSpec((None, None, dr, tf), lambda i, f: (ln, hn, i, f)),
            pl.BlockSpec((None, None, dr, tf), lambda i, f: (ln, hn, i, f)),
            pl.BlockSpec((None, None, tf, dr), lambda i, f: (ln, hn, f, i)),
        ]
        out_specs += [
            pl.BlockSpec((dr, tf), lambda i, f: (i, f)),
            pl.BlockSpec((dr, tf), lambda i, f: (i, f)),
            pl.BlockSpec((tf, dr), lambda i, f: (f, i)),
        ]
        out_shape += [jax.ShapeDtypeStruct((d, ff), BF16), jax.ShapeDtypeStruct((d, ff), BF16),
                      jax.ShapeDtypeStruct((ff, d), BF16)]
        args += [ng, nu, nd]
    outs = pl.pallas_call(
        functools.partial(_ffn_kernel, n_ff_steps=ff // tf, convert_next=next_f32 is not None),
        grid=(n_row_tiles, ff // tf),
        in_specs=in_specs,
        out_specs=out_specs,
        out_shape=out_shape,
        scratch_shapes=[pltpu.VMEM((tm, d), BF16)],
        compiler_params=_cparams(("parallel", "arbitrary")),
        name="ffn",
    )(*args)
    return outs[0], (tuple(outs[1:]) if next_f32 is not None else None)


def _mem_kv_kernel(mem_ref, g_ref, w_ref, o_ref):
    h = _rms(mem_ref[...], g_ref[...]).astype(BF16)
    o_ref[...] = _dot(h, w_ref[...]).astype(BF16)


def _mem_kv(mem2, mem_norm, w_kv):
    depth, d, n = w_kv.shape
    rows = mem2.shape[0]
    return pl.pallas_call(
        _mem_kv_kernel,
        grid=(depth,),
        in_specs=[
            pl.BlockSpec((rows, d), lambda l: (0, 0)),
            pl.BlockSpec((None, 1, d), lambda l: (l, 0, 0)),
            pl.BlockSpec((None, d, n), lambda l: (l, 0, 0)),
        ],
        out_specs=pl.BlockSpec((None, rows, n), lambda l: (l, 0, 0)),
        out_shape=jax.ShapeDtypeStruct((depth, rows, n), BF16),
        compiler_params=_cparams(("parallel",)),
        name="mem_kv",
    )(mem2, mem_norm, w_kv)


def _mem_attend(qm, mkv):
    scale = MEM_HEAD_DIM ** -0.5
    outs = []
    for hd in range(MEM_HEADS):
        lo = hd * MEM_HEAD_DIM
        qh = qm[:, lo:lo + MEM_HEAD_DIM]
        kh = mkv[:, lo:lo + MEM_HEAD_DIM]
        vh = mkv[:, MEM_WIDTH + lo:MEM_WIDTH + lo + MEM_HEAD_DIM]
        s = _dot_nt(qh, kh) * scale
        e = jnp.exp(s - jnp.max(s, axis=-1, keepdims=True))
        p = e / jnp.sum(e, axis=-1, keepdims=True)
        outs.append(_dot(p.astype(BF16), vh).astype(BF16))
    return jnp.concatenate(outs, axis=-1)


def _a_in_kernel(x_ref, g_ref, w_ref, u_ref, qm_ref):
    h = _rms(x_ref[...], g_ref[...]).astype(BF16)
    z = _dot(h, w_ref[...])
    tok = u_ref.shape[-1]
    u_ref[...] = z[:, :tok].astype(BF16)
    qm_ref[...] = z[:, tok:].astype(BF16)


def _a_in(x2, norms, w_in, layer):
    t, d = x2.shape
    n = w_in.shape[-1]
    tok = n - MEM_WIDTH
    tm = _tile(t, MIX_TOKEN_TILE)
    return pl.pallas_call(
        _a_in_kernel,
        grid=(t // tm,),
        in_specs=[
            pl.BlockSpec((tm, d), lambda i: (i, 0)),
            _const_spec((None, None, 1, d), lambda i: (layer, 2, 0, 0)),
            _const_spec((None, d, n), lambda i: (layer, 0, 0)),
        ],
        out_specs=[
            pl.BlockSpec((tm, tok), lambda i: (i, 0)),
            pl.BlockSpec((tm, MEM_WIDTH), lambda i: (i, 0)),
        ],
        out_shape=[
            jax.ShapeDtypeStruct((t, tok), BF16),
            jax.ShapeDtypeStruct((t, MEM_WIDTH), BF16),
        ],
        compiler_params=_cparams(("parallel",)),
        name="a_in",
    )(x2, norms, w_in)


def _s5_prep(lam_re, lam_im, b_re, b_im, c_re, c_im, d, log_dt, n_chunks):
    L = S5_CHUNK
    g, p = lam_re.shape
    c = b_re.shape[-1]
    g8 = S5_GROUPS_PER_TILE
    nsg = g // g8
    dt = jnp.exp(log_dt)[:, None]
    zr, zi = lam_re * dt, lam_im * dt

    def powers(mult):
        m = mult[:, None, None]
        mag = jnp.exp(m * zr[None])
        return ((mag * jnp.cos(m * zi[None])).reshape(-1, nsg, g8 * p),
                (mag * jnp.sin(m * zi[None])).reshape(-1, nsg, g8 * p))

    nsteps = max(1, int(math.ceil(math.log2(n_chunks))))
    mult = jnp.concatenate([jnp.arange(L + 1, dtype=F32), L * 2.0 ** jnp.arange(nsteps, dtype=F32)])
    pr, pi = powers(mult)
    tab = jnp.stack([pr, pi], axis=1).reshape(2 * mult.shape[0], nsg, g8 * p).transpose(1, 0, 2)

    er, ei = jnp.exp(zr) * jnp.cos(zi) - 1.0, jnp.exp(zr) * jnp.sin(zi)
    den = lam_re * lam_re + lam_im * lam_im
    fr, fi = (er * lam_re + ei * lam_im) / den, (ei * lam_re - er * lam_im) / den
    bb_r = fr[..., None] * b_re - fi[..., None] * b_im
    bb_i = fr[..., None] * b_im + fi[..., None] * b_re

    eye8 = jnp.eye(g8, dtype=F32)

    def expand(x):
        x = x.reshape(nsg, g8, c, p).transpose(0, 2, 1, 3)
        x = x[:, None] * eye8[None, :, None, :, None]
        return x.reshape(nsg, g8 * c, g8 * p)

    bt = (expand(jnp.swapaxes(bb_r, 1, 2)), expand(jnp.swapaxes(bb_i, 1, 2)))
    ct = (expand(c_re), expand(c_im))
    return bt + ct + (tab, d.reshape(nsg, 1, g8 * c))


def _s5_kernel(u_ref, btr_ref, bti_ref, ctr_ref, cti_ref, tab_ref, d_ref, y_ref,
               toep_ref, bin_ref, coutt_ref):
    L = S5_CHUNK
    nj = u_ref.shape[2]
    lanes = V7X_LANES
    half = bin_ref.shape[-1] // 2

    @pl.when(pl.program_id(1) == 0)
    def _():
        btr, bti, ctr, cti = btr_ref[...], bti_ref[...], ctr_ref[...], cti_ref[...]
        ccat = jnp.concatenate([ctr, cti], axis=1)
        toep_ref[...] = jnp.zeros(toep_ref.shape, BF16)
        for tau in range(L + 1):
            lr = tab_ref[2 * tau:2 * tau + 1, :]
            li = tab_ref[2 * tau + 1:2 * tau + 2, :]
            if tau < L:
                yr, yi = btr * lr - bti * li, btr * li + bti * lr
                s = L - 1 - tau
                bin_ref[s * lanes:(s + 1) * lanes, :half] = yr.astype(BF16)
                bin_ref[s * lanes:(s + 1) * lanes, half:] = yi.astype(BF16)
                bd = lax.dot_general(jnp.concatenate([yr, -yi], axis=1), ccat, (((1,), (1,)), ((), ())),
                                     precision=lax.Precision.HIGHEST, preferred_element_type=F32)
                if tau == 0:
                    r = lax.broadcasted_iota(jnp.int32, bd.shape, 0)
                    cidx = lax.broadcasted_iota(jnp.int32, bd.shape, 1)
                    bd = bd + jnp.where(r == cidx, d_ref[...], 0.0)
                bd = bd.astype(BF16)
                for s in range(L - tau):
                    t = s + tau
                    toep_ref[s * lanes:(s + 1) * lanes, t * lanes:(t + 1) * lanes] = bd
            if tau >= 1:
                t = tau - 1
                coutt_ref[t * lanes:(t + 1) * lanes, :half] = (ctr * lr - cti * li).astype(BF16)
                coutt_ref[t * lanes:(t + 1) * lanes, half:] = (-(ctr * li + cti * lr)).astype(BF16)

    row = lax.broadcasted_iota(jnp.int32, (nj, half), 0)

    def shifted(v, dd):
        if dd % 8 == 0:
            return jnp.concatenate([jnp.zeros((dd, v.shape[1]), v.dtype), v[:nj - dd]], axis=0)
        return jnp.where(row >= dd, pltpu.roll(v, dd, 0), 0.0)

    k0 = 2 * (L + 1)
    nsteps = (tab_ref.shape[0] - k0) // 2
    for b in range(u_ref.shape[0]):
        uu = jnp.concatenate([u_ref[b, s] for s in range(L)], axis=1)
        e = _dot(uu, bin_ref[...])
        er, ei = e[:, :half], e[:, half:]
        for k in range(nsteps):
            dd = 1 << k
            if dd >= nj:
                break
            lr = tab_ref[k0 + 2 * k:k0 + 2 * k + 1, :]
            li = tab_ref[k0 + 2 * k + 1:k0 + 2 * k + 2, :]
            sr, si = shifted(er, dd), shifted(ei, dd)
            er, ei = er + (lr * sr - li * si), ei + (lr * si + li * sr)
        xin = jnp.concatenate([shifted(er, 1), shifted(ei, 1)], axis=1).astype(BF16)
        strips = []
        for t0 in range(0, L, 2):
            kdim, cols = (t0 + 2) * lanes, slice(t0 * lanes, (t0 + 2) * lanes)
            strips.append(_dot(uu[:, :kdim], toep_ref[:kdim, cols]))
        y = jnp.concatenate(strips, axis=1) + _dot_nt(xin, coutt_ref[...])
        for t in range(L):
            y_ref[b, t] = y[:, t * lanes:(t + 1) * lanes].astype(BF16)


def _s5(u2, prep, batch):
    btr, bti, ctr, cti, tab, dd = prep
    t, tok = u2.shape
    L = S5_CHUNK
    nj = t // (batch * L)
    nsg = tok // V7X_LANES
    u4 = u2.reshape(batch, L, nj, tok)
    kk = L * V7X_LANES
    ns = btr.shape[-1]
    per_tile = lambda x, b: (x, 0, 0)
    rows = _tile(batch, S5_ROWS_PER_STEP)
    y4 = pl.pallas_call(
        _s5_kernel,
        grid=(nsg, batch // rows),
        in_specs=[pl.BlockSpec((rows, L, nj, V7X_LANES), lambda x, b: (b, 0, 0, x))]
        + [pl.BlockSpec((None, V7X_LANES, ns), per_tile)] * 4
        + [pl.BlockSpec((None, tab.shape[1], ns), per_tile), pl.BlockSpec((None, 1, V7X_LANES), per_tile)],
        out_specs=pl.BlockSpec((rows, L, nj, V7X_LANES), lambda x, b: (b, 0, 0, x)),
        out_shape=jax.ShapeDtypeStruct((batch, L, nj, tok), BF16),
        scratch_shapes=[pltpu.VMEM((kk, kk), BF16), pltpu.VMEM((kk, 2 * ns), BF16),
                        pltpu.VMEM((kk, 2 * ns), BF16)],
        compiler_params=_cparams(("parallel", "arbitrary")),
        name="s5",
    )(u4, btr, bti, ctr, cti, tab, dd)
    return y4.reshape(t, tok)


def _mix_tail(x_ref, tok_bf, qm_ref, mkv_ref, wo_ref, g_ref, o_ref):
    tok = tok_bf.shape[-1]
    mem_o = _mem_attend(qm_ref[...], mkv_ref[...])
    o = _dot(tok_bf, wo_ref[:tok, :]) + _dot(mem_o, wo_ref[tok:, :])
    o_ref[...] = x_ref[...] + _rms(o, g_ref[...])


def _mix_out_a_kernel(x_ref, y_ref, qm_ref, mkv_ref, wglu_ref, bglu_ref, wo_ref, g_ref, o_ref):
    y = jax.nn.gelu(y_ref[...].astype(F32))
    gate = jax.nn.sigmoid(_dot(y.astype(BF16), wglu_ref[...]) + bglu_ref[...])
    _mix_tail(x_ref, (y * gate).astype(BF16), qm_ref, mkv_ref, wo_ref, g_ref, o_ref)


def _mix_out_b_kernel(x_ref, a_ref, qm_ref, mkv_ref, wo_ref, g_ref, o_ref):
    _mix_tail(x_ref, a_ref[...], qm_ref, mkv_ref, wo_ref, g_ref, o_ref)


def _mix_out(x2, tok_in, qm, mkv_all, norms, w_out, layer, seq, glu=None):
    t, d = x2.shape
    tok = tok_in.shape[-1]
    tm = _tile(t, MIX_TOKEN_TILE)
    tiles_per_seq = seq // tm
    nm = mkv_all.shape[1] // (t // seq)
    row = lambda i: (i, 0)
    specs = [
        pl.BlockSpec((tm, d), row),
        pl.BlockSpec((tm, tok), row),
        pl.BlockSpec((tm, MEM_WIDTH), row),
        pl.BlockSpec((None, nm, 2 * MEM_WIDTH), lambda i: (layer, i // tiles_per_seq, 0)),
    ]
    args = [x2, tok_in, qm, mkv_all]
    if glu is not None:
        w_glu, b_glu, j = glu
        specs += [
            _const_spec((None, tok, tok), lambda i: (j, 0, 0)),
            _const_spec((None, 1, tok), lambda i: (j, 0, 0)),
        ]
        args += [w_glu, b_glu]
    specs += [
        _const_spec((None, d, d), lambda i: (layer, 0, 0)),
        _const_spec((None, None, 1, d), lambda i: (layer, 3, 0, 0)),
    ]
    args += [w_out, norms]
    return pl.pallas_call(
        _mix_out_a_kernel if glu is not None else _mix_out_b_kernel,
        grid=(t // tm,),
        in_specs=specs,
        out_specs=pl.BlockSpec((tm, d), row),
        out_shape=jax.ShapeDtypeStruct((t, d), F32),
        compiler_params=_cparams(("parallel",)),
        name="mix_out_a" if glu is not None else "mix_out_b",
    )(*args)


def _kv_kernel(x_ref, pos_ref, frq_ref, sgn_ref, gin_ref, wd_ref, gkv_ref, wuk_ref, wuv_ref, wkr_ref,
               k_ref, v_ref, rc_ref, rs_ref):
    ang = pos_ref[...].astype(F32) * frq_ref[...]
    rc = jnp.cos(ang)
    rs = jnp.sin(ang) * sgn_ref[...]
    rc_ref[...] = rc
    rs_ref[...] = rs
    h = _rms(x_ref[...], gin_ref[...]).astype(BF16)
    ckv = _rms(_dot(h, wd_ref[...]), gkv_ref[...]).astype(BF16)
    kn = _dot(ckv, wuk_ref[...])
    v_ref[...] = _dot(ckv, wuv_ref[...]).astype(BF16)
    kr2 = _dot(h, wkr_ref[...])
    kr = (kr2[:, :V7X_LANES] * rc + kr2[:, V7X_LANES:] * rs).astype(BF16)
    heads = k_ref.shape[-1] // Q_SLOT
    for hh in range(heads):
        k_ref[:, hh * Q_SLOT:hh * Q_SLOT + MLA_NOPE] = kn[:, hh * MLA_NOPE:(hh + 1) * MLA_NOPE].astype(BF16)
        k_ref[:, hh * Q_SLOT + MLA_NOPE:(hh + 1) * Q_SLOT] = kr


def _kv(x2, pos2, frq, sgn, g_in, w_dkv, g_kv, w_uk, w_uv, w_kr2):
    t, d = x2.shape
    r = w_dkv.shape[-1]
    hv = w_uv.shape[-1]
    heads = hv // MLA_V
    tm = _tile(t, MIX_TOKEN_TILE)
    row = lambda i: (i, 0)
    c2 = lambda i: (0, 0)
    return pl.pallas_call(
        _kv_kernel,
        grid=(t // tm,),
        in_specs=[
            pl.BlockSpec((tm, d), row),
            pl.BlockSpec((tm, 1), row),
            _const_spec((1, V7X_LANES), c2),
            _const_spec((1, V7X_LANES), c2),
            _const_spec((1, d), c2),
            _const_spec((d, r), c2),
            _const_spec((1, r), c2),
            _const_spec((r, heads * MLA_NOPE), c2),
            _const_spec((r, hv), c2),
            _const_spec((d, 2 * V7X_LANES), c2),
        ],
        out_specs=[
            pl.BlockSpec((tm, heads * Q_SLOT), row),
            pl.BlockSpec((tm, hv), row),
            pl.BlockSpec((tm, V7X_LANES), row),
            pl.BlockSpec((tm, V7X_LANES), row),
        ],
        out_shape=[
            jax.ShapeDtypeStruct((t, heads * Q_SLOT), BF16),
            jax.ShapeDtypeStruct((t, hv), BF16),
            jax.ShapeDtypeStruct((t, V7X_LANES), F32),
            jax.ShapeDtypeStruct((t, V7X_LANES), F32),
        ],
        compiler_params=_cparams(("parallel",)),
        name="kv",
    )(x2, pos2, frq, sgn, g_in, w_dkv, g_kv, w_uk, w_uv, w_kr2)


def _b_in_kernel(x_ref, rc_ref, rs_ref, g_ref, w_ref, gq_ref, wq_ref, q_ref, qm_ref):
    h = _rms(x_ref[...], g_ref[...]).astype(BF16)
    z = _dot(h, w_ref[...])
    rank = gq_ref.shape[-1]
    qm_ref[...] = z[:, rank:].astype(BF16)
    cq = _rms(z[:, :rank], gq_ref[...]).astype(BF16)
    zq = _dot(cq, wq_ref[...])
    heads = q_ref.shape[-1] // Q_SLOT
    qs = (MLA_NOPE + MLA_ROPE) ** -0.5 * LOG2E
    rc, rs = rc_ref[...] * qs, rs_ref[...] * qs
    sw0 = heads * Q_SLOT
    for hh in range(heads):
        lo = hh * Q_SLOT
        q_ref[:, lo:lo + MLA_NOPE] = (zq[:, lo:lo + MLA_NOPE] * qs).astype(BF16)
        main = zq[:, lo + MLA_NOPE:lo + Q_SLOT]
        swp = zq[:, sw0 + hh * V7X_LANES:sw0 + (hh + 1) * V7X_LANES]
        q_ref[:, lo + MLA_NOPE:lo + Q_SLOT] = (main * rc + swp * rs).astype(BF16)


def _b_in(x2, rc, rs, norms, w_in, g_q, w_q, layer, j):
    t, d = x2.shape
    n = w_in.shape[-1]
    rank = g_q.shape[-1]
    nq = w_q.shape[-1]
    heads = nq // (Q_SLOT + V7X_LANES)
    tm = _tile(t, MIX_TOKEN_TILE)
    row = lambda i: (i, 0)
    return pl.pallas_call(
        _b_in_kernel,
        grid=(t // tm,),
        in_specs=[
            pl.BlockSpec((tm, d), row),
            pl.BlockSpec((tm, V7X_LANES), row),
            pl.BlockSpec((tm, V7X_LANES), row),
            _const_spec((None, None, 1, d), lambda i: (layer, 2, 0, 0)),
            _const_spec((None, d, n), lambda i: (j, 0, 0)),
            _const_spec((None, 1, rank), lambda i: (j, 0, 0)),
            _const_spec((None, rank, nq), lambda i: (j, 0, 0)),
        ],
        out_specs=[
            pl.BlockSpec((tm, heads * Q_SLOT), row),
            pl.BlockSpec((tm, MEM_WIDTH), row),
        ],
        out_shape=[
            jax.ShapeDtypeStruct((t, heads * Q_SLOT), BF16),
            jax.ShapeDtypeStruct((t, MEM_WIDTH), BF16),
        ],
        compiler_params=_cparams(("parallel",)),
        name="b_in",
    )(x2, rc, rs, norms, w_in, g_q, w_q)


ATTN_BLOCK = 512


def _attn_kernel(q_ref, k_ref, v_ref, o_ref, vx_ref, s_ref, *, blk):
    kv = 2 * blk
    n_qblocks = q_ref.shape[0] // kv

    vx_ref[:, :MLA_V] = v_ref[...]
    lane = lax.broadcasted_iota(jnp.int32, (vx_ref.shape[0], vx_ref.shape[1] - MLA_V), 1)
    vx_ref[:, MLA_V:] = jnp.where(lane == 0, 1.0, 0.0).astype(BF16)

    def keys_of(j, n):
        return pl.ds(j * kv, n)

    def scores(i, chain, j):
        n_keys = blk if (chain == 0 and j == i) else kv
        q_rows = pl.ds(i * kv + chain * blk, blk)
        s_ref[chain, :, :n_keys] = _dot_nt(q_ref[q_rows, :], k_ref[keys_of(j, n_keys), :])

    def absorb(chain, j, carry, n_keys, diag_offset=None):
        m, acc = carry
        s = s_ref[chain, :, :n_keys]
        if diag_offset is not None:
            rows = lax.broadcasted_iota(jnp.int32, s.shape, 0)
            cols = lax.broadcasted_iota(jnp.int32, s.shape, 1)
            s = jnp.where(cols <= rows + diag_offset, s, -jnp.inf)
        m_new = jnp.maximum(m, jnp.max(s, axis=-1, keepdims=True))
        p = jnp.exp2(s - m_new).astype(BF16)
        acc = jnp.exp2(m - m_new) * acc + _dot(p, vx_ref[keys_of(j, n_keys), :])
        return m_new, acc

    init = (jnp.full((blk, 1), -jnp.inf, F32), jnp.zeros((blk, vx_ref.shape[1]), F32))
    scores(0, 0, 0)
    for i in range(n_qblocks):
        ca, cb = init, init
        for j in range(i):
            scores(i, 1, j)
            ca = absorb(0, j, ca, kv)
            scores(i, 0, j + 1)
            cb = absorb(1, j, cb, kv)
        scores(i, 1, i)
        ca = absorb(0, i, ca, blk, diag_offset=0)
        if i + 1 < n_qblocks:
            scores(i + 1, 0, 0)
        cb = absorb(1, i, cb, kv, diag_offset=blk)
        for chain, (_, acc) in enumerate((ca, cb)):
            out_rows = pl.ds(i * kv + chain * blk, blk)
            o_ref[out_rows, :] = (acc[:, :MLA_V] / acc[:, MLA_V:MLA_V + 1]).astype(BF16)


def _attn(q, k_all, v, batch, seq):
    t = q.shape[0]
    heads = v.shape[-1] // MLA_V
    blk = _tile(seq // 2, ATTN_BLOCK)
    per_head = lambda b, h: (b, h)
    return pl.pallas_call(
        functools.partial(_attn_kernel, blk=blk),
        grid=(batch, heads),
        in_specs=[
            pl.BlockSpec((seq, Q_SLOT), per_head),
            pl.BlockSpec((seq, Q_SLOT), per_head),
            pl.BlockSpec((seq, MLA_V), per_head),
        ],
        out_specs=pl.BlockSpec((seq, MLA_V), per_head),
        out_shape=jax.ShapeDtypeStruct((t, heads * MLA_V), BF16),
        scratch_shapes=[pltpu.VMEM((seq, 2 * MLA_V), BF16), pltpu.VMEM((2, blk, 2 * blk), F32)],
        compiler_params=_cparams(("parallel", "parallel")),
        name="attn",
    )(q, k_all, v)


def _pad_cols(w, n):
    return jnp.pad(w, ((0, 0),) * (w.ndim - 1) + ((0, n - w.shape[-1]),))


def _q_weight(w_uq):
    nb, r, n = w_uq.shape
    heads = n // (MLA_NOPE + MLA_ROPE)
    w = w_uq.reshape(nb, r, heads, MLA_NOPE + MLA_ROPE)
    half = MLA_ROPE // 2
    nope, t1, t2 = w[..., :MLA_NOPE], w[..., MLA_NOPE:MLA_NOPE + half], w[..., MLA_NOPE + half:]
    main = _pad_cols(jnp.concatenate([nope, t1, t2], axis=-1), Q_SLOT).reshape(nb, r, heads * Q_SLOT)
    swp = _pad_cols(jnp.concatenate([t2, t1], axis=-1), V7X_LANES).reshape(nb, r, heads * V7X_LANES)
    return jnp.concatenate([main, swp], axis=-1)


def kernel(x, mem, positions, norms, ffn_w_gate, ffn_w_up, ffn_w_down, w_out, mem_norm, mem_w_kv, a_w_in, s5_lambda_re, s5_lambda_im, s5_b_re, s5_b_im, s5_c_re, s5_c_im, s5_d, s5_log_dt, s5_w_glu, s5_b_glu, b_w_in, mla_q_norm, mla_w_uq, kv_in_norm, w_dkv, kv_norm, w_uk, w_uv, w_kr):
    bsz, seq, d = x.shape
    t = bsz * seq
    depth = norms.shape[0]
    n_a = a_w_in.shape[0]
    n_b = b_w_in.shape[0]
    L = S5_CHUNK
    nj = seq // L
    half = MLA_ROPE // 2

    norms4 = norms.reshape(depth, norms.shape[1], 1, d)
    w_out_b = w_out.astype(BF16)

    ffn_state = {"w": (ffn_w_gate[0, 0].astype(BF16), ffn_w_up[0, 0].astype(BF16),
                       ffn_w_down[0, 0].astype(BF16))}

    def ffn(xx, layer, hf):
        nxt = (layer, 1) if hf == 0 else (layer + 1, 0)
        next_f32 = (ffn_w_gate, ffn_w_up, ffn_w_down) + nxt if nxt[0] < depth else None
        xx, ffn_state["w"] = _ffn(xx, norms4, ffn_state["w"], layer, hf, next_f32)
        return xx

    mkv_all = _mem_kv(mem.reshape(bsz * mem.shape[1], d), mem_norm.reshape(depth, 1, d),
                      mem_w_kv.astype(BF16))

    xp = x.reshape(bsz, nj, L, d).transpose(0, 2, 1, 3).reshape(t, d)
    a_w_in_b = a_w_in.astype(BF16)
    w_glu_b = s5_w_glu.astype(BF16)
    b_glu3 = s5_b_glu.reshape(n_a, 1, -1)
    for l in range(n_a):
        prep = _s5_prep(s5_lambda_re[l], s5_lambda_im[l], s5_b_re[l], s5_b_im[l], s5_c_re[l], s5_c_im[l],
                        s5_d[l], s5_log_dt[l], nj)
        xp = ffn(xp, l, 0)
        u, qm = _a_in(xp, norms4, a_w_in_b, l)
        y = _s5(u, prep, bsz)
        xp = _mix_out(xp, y, qm, mkv_all, norms4, w_out_b, l, seq, glu=(w_glu_b, b_glu3, l))
        xp = ffn(xp, l, 1)
    x2 = xp.reshape(bsz, L, nj, d).transpose(0, 2, 1, 3).reshape(t, d)

    inv_freq = ROPE_THETA ** (-jnp.arange(0, MLA_ROPE, 2, dtype=F32) / MLA_ROPE)
    zeros = jnp.zeros((V7X_LANES - MLA_ROPE,), F32)
    frq = jnp.concatenate([inv_freq, inv_freq, zeros]).reshape(1, V7X_LANES)
    sgn = jnp.concatenate([-jnp.ones((half,), F32), jnp.ones((half,), F32), zeros]).reshape(1, V7X_LANES)
    kr1, kr2 = w_kr[:, :half], w_kr[:, half:]
    w_kr2 = jnp.concatenate([_pad_cols(w_kr, V7X_LANES),
                             _pad_cols(jnp.concatenate([kr2, kr1], axis=-1), V7X_LANES)], axis=-1).astype(BF16)
    k_all, v, rc, rs = _kv(x2, positions.reshape(t, 1), frq, sgn, kv_in_norm.reshape(1, d),
                           w_dkv.astype(BF16), kv_norm.reshape(1, -1), w_uk.astype(BF16),
                           w_uv.astype(BF16), w_kr2)

    b_w_in_b = b_w_in.astype(BF16)
    w_q = _q_weight(mla_w_uq).astype(BF16)
    g_q = mla_q_norm.reshape(n_b, 1, -1)
    for j in range(n_b):
        l = n_a + j
        x2 = ffn(x2, l, 0)
        q, qm = _b_in(x2, rc, rs, norms4, b_w_in_b, g_q, w_q, l, j)
        o = _attn(q, k_all, v, bsz, seq)
        x2 = _mix_out(x2, o, qm, mkv_all, norms4, w_out_b, l, seq)
        x2 = ffn(x2, l, 1)
    return x2.reshape(bsz, seq, d)
```

```python
import functools
import math

import jax
import jax.numpy as jnp
from jax import lax
from jax.experimental import pallas as pl
from jax.experimental.pallas import tpu as pltpu

F32 = jnp.float32
BF16 = jnp.bfloat16

EPS = 1e-6
ROPE_THETA = 10000.0
LOG2E = 1.4426950408889634

V7X_LANES = 128
V7X_VMEM_BYTES = 64 * 1024 * 1024
VMEM_LIMIT = V7X_VMEM_BYTES - 6 * 1024 * 1024

S5_GROUP = 16
MLA_NOPE = 128
MLA_ROPE = 64
MLA_V = 128
MEM_HEADS = 4
MEM_HEAD_DIM = 128
MEM_WIDTH = MEM_HEADS * MEM_HEAD_DIM

S5_CHUNK = 8
S5_ROWS_PER_STEP = 2
S5_GROUPS_PER_TILE = V7X_LANES // S5_GROUP
Q_SLOT = 2 * V7X_LANES

FFN_TOKEN_TILE = 1024
FFN_FF_TILE = 512
MIX_TOKEN_TILE = 512


def _tile(n, want):
    t = min(n, want)
    assert n % t == 0, (n, want)
    return t


def _cparams(sem):
    return pltpu.CompilerParams(dimension_semantics=sem, vmem_limit_bytes=VMEM_LIMIT)


def _const_spec(shape, index_map):
    return pl.BlockSpec(shape, index_map, pipeline_mode=pl.Buffered(1))


def _rms(xf, g):
    ms = jnp.mean(xf * xf, axis=-1, keepdims=True)
    return xf * lax.rsqrt(ms + EPS) * g


def _dot(a, b):
    return jnp.dot(a, b, preferred_element_type=F32)


def _dot_nt(a, b):
    return lax.dot_general(a, b, (((1,), (1,)), ((), ())), preferred_element_type=F32)


FFN_ROW_CHUNK = 512


def _ffn_kernel(*refs, n_ff_steps, convert_next):
    if convert_next:
        (x_ref, g0_ref, g1_ref, wg_ref, wu_ref, wd_ref, ng_ref, nu_ref, nd_ref,
         o_ref, cg_ref, cu_ref, cd_ref, h_ref) = refs
        cg_ref[...] = ng_ref[...].astype(BF16)
        cu_ref[...] = nu_ref[...].astype(BF16)
        cd_ref[...] = nd_ref[...].astype(BF16)
    else:
        x_ref, g0_ref, g1_ref, wg_ref, wu_ref, wd_ref, o_ref, h_ref = refs
    f = pl.program_id(1)
    rc = min(FFN_ROW_CHUNK, x_ref.shape[0])
    n_chunks = x_ref.shape[0] // rc

    def step(first, final):
        for r in range(n_chunks):
            rows = pl.ds(r * rc, rc)
            if first:
                h = _rms(x_ref[rows, :], g0_ref[...]).astype(BF16)
                h_ref[rows, :] = h
            else:
                h = h_ref[rows, :]
            g = _dot(h, wg_ref[...])
            u = _dot(h, wu_ref[...])
            a = (g * jax.nn.sigmoid(g) * u).astype(BF16)
            o = _dot(a, wd_ref[...])
            if not first:
                o = o_ref[rows, :] + o
            if final:
                o = x_ref[rows, :] + 0.5 * _rms(o, g1_ref[...])
            o_ref[rows, :] = o

    if n_ff_steps == 1:
        step(True, True)
        return
    pl.when(f == 0)(lambda: step(True, False))
    if n_ff_steps > 2:
        pl.when(jnp.logical_and(f > 0, f < n_ff_steps - 1))(lambda: step(False, False))
    pl.when(f == n_ff_steps - 1)(lambda: step(False, True))


def _ffn(x2, norms, weights, layer, half, next_f32=None):
    wg, wu, wd = weights
    t, d = x2.shape
    ff = wg.shape[-1]
    tm = _tile(t, FFN_TOKEN_TILE)
    tf = _tile(ff, FFN_FF_TILE)
    n_row_tiles = t // tm
    n0, n1 = (0, 1) if half == 0 else (4, 5)
    in_specs = [
        pl.BlockSpec((tm, d), lambda i, f: (i, 0)),
        pl.BlockSpec((None, None, 1, d), lambda i, f: (layer, n0, 0, 0)),
        pl.BlockSpec((None, None, 1, d), lambda i, f: (layer, n1, 0, 0)),
        pl.BlockSpec((d, tf), lambda i, f: (0, f)),
        pl.BlockSpec((d, tf), lambda i, f: (0, f)),
        pl.BlockSpec((tf, d), lambda i, f: (f, 0)),
    ]
    out_specs = [pl.BlockSpec((tm, d), lambda i, f: (i, 0))]
    out_shape = [jax.ShapeDtypeStruct((t, d), F32)]
    args = [x2, norms, norms, wg, wu, wd]
    if next_f32 is not None:
        ng, nu, nd, ln, hn = next_f32
        dr = d // n_row_tiles
        in_specs += [
            pl.BlockSpec((None, None, dr, tf), lambda i, f: (ln, hn, i, f)),
            pl.BlockSpec((None, None, dr, tf), lambda i, f: (ln, hn, i, f)),
            pl.BlockSpec((None, None, tf, dr), lambda i, f: (ln, hn, f, i)),
        ]
        out_specs += [
            pl.BlockSpec((dr, tf), lambda i, f: (i, f)),
            pl.BlockSpec((dr, tf), lambda i, f: (i, f)),
            pl.BlockSpec((tf, dr), lambda i, f: (f, i)),
        ]
        out_shape += [jax.ShapeDtypeStruct((d, ff), BF16), jax.ShapeDtypeStruct((d, ff), BF16),
                      jax.ShapeDtypeStruct((ff, d), BF16)]
        args += [ng, nu, nd]
    outs = pl.pallas_call(
        functools.partial(_ffn_kernel, n_ff_steps=ff // tf, convert_next=next_f32 is not None),
        grid=(n_row_tiles, ff // tf),
        in_specs=in_specs,
        out_specs=out_specs,
        out_shape=out_shape,
        scratch_shapes=[pltpu.VMEM((tm, d), BF16)],
        compiler_params=_cparams(("parallel", "arbitrary")),
        name="ffn",
    )(*args)
    return outs[0], (tuple(outs[1:]) if next_f32 is not None else None)


def _mem_kv_kernel(mem_ref, g_ref, w_ref, o_ref):
    h = _rms(mem_ref[...], g_ref[...]).astype(BF16)
    o_ref[...] = _dot(h, w_ref[...]).astype(BF16)


def _mem_kv(mem2, mem_norm, w_kv):
    depth, d, n = w_kv.shape
    rows = mem2.shape[0]
    return pl.pallas_call(
        _mem_kv_kernel,
        grid=(depth,),
        in_specs=[
            pl.BlockSpec((rows, d), lambda l: (0, 0)),
            pl.BlockSpec((None, 1, d), lambda l: (l, 0, 0)),
            pl.BlockSpec((None, d, n), lambda l: (l, 0, 0)),
        ],
        out_specs=pl.BlockSpec((None, rows, n), lambda l: (l, 0, 0)),
        out_shape=jax.ShapeDtypeStruct((depth, rows, n), BF16),
        compiler_params=_cparams(("parallel",)),
        name="mem_kv",
    )(mem2, mem_norm, w_kv)


def _mem_attend(qm, mkv):
    scale = MEM_HEAD_DIM ** -0.5
    outs = []
    for hd in range(MEM_HEADS):
        lo = hd * MEM_HEAD_DIM
        qh = qm[:, lo:lo + MEM_HEAD_DIM]
        kh = mkv[:, lo:lo + MEM_HEAD_DIM]
        vh = mkv[:, MEM_WIDTH + lo:MEM_WIDTH + lo + MEM_HEAD_DIM]
        s = _dot_nt(qh, kh) * scale
        e = jnp.exp(s - jnp.max(s, axis=-1, keepdims=True))
        p = e / jnp.sum(e, axis=-1, keepdims=True)
        outs.append(_dot(p.astype(BF16), vh).astype(BF16))
    return jnp.concatenate(outs, axis=-1)


def _a_in_kernel(x_ref, g_ref, w_ref, u_ref, qm_ref):
    h = _rms(x_ref[...], g_ref[...]).astype(BF16)
    z = _dot(h, w_ref[...])
    tok = u_ref.shape[-1]
    u_ref[...] = z[:, :tok].astype(BF16)
    qm_ref[...] = z[:, tok:].astype(BF16)


def _a_in(x2, norms, w_in, layer):
    t, d = x2.shape
    n = w_in.shape[-1]
    tok = n - MEM_WIDTH
    tm = _tile(t, MIX_TOKEN_TILE)
    return pl.pallas_call(
        _a_in_kernel,
        grid=(t // tm,),
        in_specs=[
            pl.BlockSpec((tm, d), lambda i: (i, 0)),
            _const_spec((None, None, 1, d), lambda i: (layer, 2, 0, 0)),
            _const_spec((None, d, n), lambda i: (layer, 0, 0)),
        ],
        out_specs=[
            pl.BlockSpec((tm, tok), lambda i: (i, 0)),
            pl.BlockSpec((tm, MEM_WIDTH), lambda i: (i, 0)),
        ],
        out_shape=[
            jax.ShapeDtypeStruct((t, tok), BF16),
            jax.ShapeDtypeStruct((t, MEM_WIDTH), BF16),
        ],
        compiler_params=_cparams(("parallel",)),
        name="a_in",
    )(x2, norms, w_in)


def _s5_prep(lam_re, lam_im, b_re, b_im, c_re, c_im, d, log_dt, n_chunks):
    L = S5_CHUNK
    g, p = lam_re.shape
    c = b_re.shape[-1]
    g8 = S5_GROUPS_PER_TILE
    nsg = g // g8
    dt = jnp.exp(log_dt)[:, None]
    zr, zi = lam_re * dt, lam_im * dt

    def powers(mult):
        m = mult[:, None, None]
        mag = jnp.exp(m * zr[None])
        return ((mag * jnp.cos(m * zi[None])).reshape(-1, nsg, g8 * p),
                (mag * jnp.sin(m * zi[None])).reshape(-1, nsg, g8 * p))

    nsteps = max(1, int(math.ceil(math.log2(n_chunks))))
    mult = jnp.concatenate([jnp.arange(L + 1, dtype=F32), L * 2.0 ** jnp.arange(nsteps, dtype=F32)])
    pr, pi = powers(mult)
    tab = jnp.stack([pr, pi], axis=1).reshape(2 * mult.shape[0], nsg, g8 * p).transpose(1, 0, 2)

    er, ei = jnp.exp(zr) * jnp.cos(zi) - 1.0, jnp.exp(zr) * jnp.sin(zi)
    den = lam_re * lam_re + lam_im * lam_im
    fr, fi = (er * lam_re + ei * lam_im) / den, (ei * lam_re - er * lam_im) / den
    bb_r = fr[..., None] * b_re - fi[..., None] * b_im
    bb_i = fr[..., None] * b_im + fi[..., None] * b_re

    eye8 = jnp.eye(g8, dtype=F32)

    def expand(x):
        x = x.reshape(nsg, g8, c, p).transpose(0, 2, 1, 3)
        x = x[:, None] * eye8[None, :, None, :, None]
        return x.reshape(nsg, g8 * c, g8 * p)

    bt = (expand(jnp.swapaxes(bb_r, 1, 2)), expand(jnp.swapaxes(bb_i, 1, 2)))
    ct = (expand(c_re), expand(c_im))
    return bt + ct + (tab, d.reshape(nsg, 1, g8 * c))


def _s5_kernel(u_ref, btr_ref, bti_ref, ctr_ref, cti_ref, tab_ref, d_ref, y_ref,
               toep_ref, bin_ref, coutt_ref):
    L = S5_CHUNK
    nj = u_ref.shape[2]
    lanes = V7X_LANES
    half = bin_ref.shape[-1] // 2

    @pl.when(pl.program_id(1) == 0)
    def _():
        btr, bti, ctr, cti = btr_ref[...], bti_ref[...], ctr_ref[...], cti_ref[...]
        ccat = jnp.concatenate([ctr, cti], axis=1)
        toep_ref[...] = jnp.zeros(toep_ref.shape, BF16)
        for tau in range(L + 1):
            lr = tab_ref[2 * tau:2 * tau + 1, :]
            li = tab_ref[2 * tau + 1:2 * tau + 2, :]
            if tau < L:
                yr, yi = btr * lr - bti * li, btr * li + bti * lr
                s = L - 1 - tau
                bin_ref[s * lanes:(s + 1) * lanes, :half] = yr.astype(BF16)
                bin_ref[s * lanes:(s + 1) * lanes, half:] = yi.astype(BF16)
                bd = lax.dot_general(jnp.concatenate([yr, -yi], axis=1), ccat, (((1,), (1,)), ((), ())),
                                     precision=lax.Precision.HIGHEST, preferred_element_type=F32)
                if tau == 0:
                    r = lax.broadcasted_iota(jnp.int32, bd.shape, 0)
                    cidx = lax.broadcasted_iota(jnp.int32, bd.shape, 1)
                    bd = bd + jnp.where(r == cidx, d_ref[...], 0.0)
                bd = bd.astype(BF16)
                for s in range(L - tau):
                    t = s + tau
                    toep_ref[s * lanes:(s + 1) * lanes, t * lanes:(t + 1) * lanes] = bd
            if tau >= 1:
                t = tau - 1
                coutt_ref[t * lanes:(t + 1) * lanes, :half] = (ctr * lr - cti * li).astype(BF16)
                coutt_ref[t * lanes:(t + 1) * lanes, half:] = (-(ctr * li + cti * lr)).astype(BF16)

    row = lax.broadcasted_iota(jnp.int32, (nj, half), 0)

    def shifted(v, dd):
        if dd % 8 == 0:
            return jnp.concatenate([jnp.zeros((dd, v.shape[1]), v.dtype), v[:nj - dd]], axis=0)
        return jnp.where(row >= dd, pltpu.roll(v, dd, 0), 0.0)

    k0 = 2 * (L + 1)
    nsteps = (tab_ref.shape[0] - k0) // 2
    for b in range(u_ref.shape[0]):
        uu = jnp.concatenate([u_ref[b, s] for s in range(L)], axis=1)
        e = _dot(uu, bin_ref[...])
        er, ei = e[:, :half], e[:, half:]
        for k in range(nsteps):
            dd = 1 << k
            if dd >= nj:
                break
            lr = tab_ref[k0 + 2 * k:k0 + 2 * k + 1, :]
            li = tab_ref[k0 + 2 * k + 1:k0 + 2 * k + 2, :]
            sr, si = shifted(er, dd), shifted(ei, dd)
            er, ei = er + (lr * sr - li * si), ei + (lr * si + li * sr)
        xin = jnp.concatenate([shifted(er, 1), shifted(ei, 1)], axis=1).astype(BF16)
        strips = []
        for t0 in range(0, L, 2):
            kdim, cols = (t0 + 2) * lanes, slice(t0 * lanes, (t0 + 2) * lanes)
            strips.append(_dot(uu[:, :kdim], toep_ref[:kdim, cols]))
        y = jnp.concatenate(strips, axis=1) + _dot_nt(xin, coutt_ref[...])
        for t in range(L):
            y_ref[b, t] = y[:, t * lanes:(t + 1) * lanes].astype(BF16)


def _s5(u2, prep, batch):
    btr, bti, ctr, cti, tab, dd = prep
    t, tok = u2.shape
    L = S5_CHUNK
    nj = t // (batch * L)
    nsg = tok // V7X_LANES
    u4 = u2.reshape(batch, L, nj, tok)
    kk = L * V7X_LANES
    ns = btr.shape[-1]
    per_tile = lambda x, b: (x, 0, 0)
    rows = _tile(batch, S5_ROWS_PER_STEP)
    y4 = pl.pallas_call(
        _s5_kernel,
        grid=(nsg, batch // rows),
        in_specs=[pl.BlockSpec((rows, L, nj, V7X_LANES), lambda x, b: (b, 0, 0, x))]
        + [pl.BlockSpec((None, V7X_LANES, ns), per_tile)] * 4
        + [pl.BlockSpec((None, tab.shape[1], ns), per_tile), pl.BlockSpec((None, 1, V7X_LANES), per_tile)],
        out_specs=pl.BlockSpec((rows, L, nj, V7X_LANES), lambda x, b: (b, 0, 0, x)),
        out_shape=jax.ShapeDtypeStruct((batch, L, nj, tok), BF16),
        scratch_shapes=[pltpu.VMEM((kk, kk), BF16), pltpu.VMEM((kk, 2 * ns), BF16),
                        pltpu.VMEM((kk, 2 * ns), BF16)],
        compiler_params=_cparams(("parallel", "arbitrary")),
        name="s5",
    )(u4, btr, bti, ctr, cti, tab, dd)
    return y4.reshape(t, tok)


def _mix_tail(x_ref, tok_bf, qm_ref, mkv_ref, wo_ref, g_ref, o_ref):
    tok = tok_bf.shape[-1]
    mem_o = _mem_attend(qm_ref[...], mkv_ref[...])
    o = _dot(tok_bf, wo_ref[:tok, :]) + _dot(mem_o, wo_ref[tok:, :])
    o_ref[...] = x_ref[...] + _rms(o, g_ref[...])


def _mix_out_a_kernel(x_ref, y_ref, qm_ref, mkv_ref, wglu_ref, bglu_ref, wo_ref, g_ref, o_ref):
    y = jax.nn.gelu(y_ref[...].astype(F32))
    gate = jax.nn.sigmoid(_dot(y.astype(BF16), wglu_ref[...]) + bglu_ref[...])
    _mix_tail(x_ref, (y * gate).astype(BF16), qm_ref, mkv_ref, wo_ref, g_ref, o_ref)


def _mix_out_b_kernel(x_ref, a_ref, qm_ref, mkv_ref, wo_ref, g_ref, o_ref):
    _mix_tail(x_ref, a_ref[...], qm_ref, mkv_ref, wo_ref, g_ref, o_ref)


def _mix_out(x2, tok_in, qm, mkv_all, norms, w_out, layer, seq, glu=None):
    t, d = x2.shape
    tok = tok_in.shape[-1]
    tm = _tile(t, MIX_TOKEN_TILE)
    tiles_per_seq = seq // tm
    nm = mkv_all.shape[1] // (t // seq)
    row = lambda i: (i, 0)
    specs = [
        pl.BlockSpec((tm, d), row),
        pl.BlockSpec((tm, tok), row),
        pl.BlockSpec((tm, MEM_WIDTH), row),
        pl.BlockSpec((None, nm, 2 * MEM_WIDTH), lambda i: (layer, i // tiles_per_seq, 0)),
    ]
    args = [x2, tok_in, qm, mkv_all]
    if glu is not None:
        w_glu, b_glu, j = glu
        specs += [
            _const_spec((None, tok, tok), lambda i: (j, 0, 0)),
            _const_spec((None, 1, tok), lambda i: (j, 0, 0)),
        ]
        args += [w_glu, b_glu]
    specs += [
        _const_spec((None, d, d), lambda i: (layer, 0, 0)),
        _const_spec((None, None, 1, d), lambda i: (layer, 3, 0, 0)),
    ]
    args += [w_out, norms]
    return pl.pallas_call(
        _mix_out_a_kernel if glu is not None else _mix_out_b_kernel,
        grid=(t // tm,),
        in_specs=specs,
        out_specs=pl.BlockSpec((tm, d), row),
        out_shape=jax.ShapeDtypeStruct((t, d), F32),
        compiler_params=_cparams(("parallel",)),
        name="mix_out_a" if glu is not None else "mix_out_b",
    )(*args)


def _kv_kernel(x_ref, pos_ref, frq_ref, sgn_ref, gin_ref, wd_ref, gkv_ref, wuk_ref, wuv_ref, wkr_ref,
               k_ref, v_ref, rc_ref, rs_ref):
    ang = pos_ref[...].astype(F32) * frq_ref[...]
    rc = jnp.cos(ang)
    rs = jnp.sin(ang) * sgn_ref[...]
    rc_ref[...] = rc
    rs_ref[...] = rs
    h = _rms(x_ref[...], gin_ref[...]).astype(BF16)
    ckv = _rms(_dot(h, wd_ref[...]), gkv_ref[...]).astype(BF16)
    kn = _dot(ckv, wuk_ref[...])
    v_ref[...] = _dot(ckv, wuv_ref[...]).astype(BF16)
    kr2 = _dot(h, wkr_ref[...])
    kr = (kr2[:, :V7X_LANES] * rc + kr2[:, V7X_LANES:] * rs).astype(BF16)
    heads = k_ref.shape[-1] // Q_SLOT
    for hh in range(heads):
        k_ref[:, hh * Q_SLOT:hh * Q_SLOT + MLA_NOPE] = kn[:, hh * MLA_NOPE:(hh + 1) * MLA_NOPE].astype(BF16)
        k_ref[:, hh * Q_SLOT + MLA_NOPE:(hh + 1) * Q_SLOT] = kr


def _kv(x2, pos2, frq, sgn, g_in, w_dkv, g_kv, w_uk, w_uv, w_kr2):
    t, d = x2.shape
    r = w_dkv.shape[-1]
    hv = w_uv.shape[-1]
    heads = hv // MLA_V
    tm = _tile(t, MIX_TOKEN_TILE)
    row = lambda i: (i, 0)
    c2 = lambda i: (0, 0)
    return pl.pallas_call(
        _kv_kernel,
        grid=(t // tm,),
        in_specs=[
            pl.BlockSpec((tm, d), row),
            pl.BlockSpec((tm, 1), row),
            _const_spec((1, V7X_LANES), c2),
            _const_spec((1, V7X_LANES), c2),
            _const_spec((1, d), c2),
            _const_spec((d, r), c2),
            _const_spec((1, r), c2),
            _const_spec((r, heads * MLA_NOPE), c2),
            _const_spec((r, hv), c2),
            _const_spec((d, 2 * V7X_LANES), c2),
        ],
        out_specs=[
            pl.BlockSpec((tm, heads * Q_SLOT), row),
            pl.BlockSpec((tm, hv), row),
            pl.BlockSpec((tm, V7X_LANES), row),
            pl.BlockSpec((tm, V7X_LANES), row),
        ],
        out_shape=[
            jax.ShapeDtypeStruct((t, heads * Q_SLOT), BF16),
            jax.ShapeDtypeStruct((t, hv), BF16),
            jax.ShapeDtypeStruct((t, V7X_LANES), F32),
            jax.ShapeDtypeStruct((t, V7X_LANES), F32),
        ],
        compiler_params=_cparams(("parallel",)),
        name="kv",
    )(x2, pos2, frq, sgn, g_in, w_dkv, g_kv, w_uk, w_uv, w_kr2)


def _b_in_kernel(x_ref, rc_ref, rs_ref, g_ref, w_ref, gq_ref, wq_ref, q_ref, qm_ref):
    h = _rms(x_ref[...], g_ref[...]).astype(BF16)
    z = _dot(h, w_ref[...])
    rank = gq_ref.shape[-1]
    qm_ref[...] = z[:, rank:].astype(BF16)
    cq = _rms(z[:, :rank], gq_ref[...]).astype(BF16)
    zq = _dot(cq, wq_ref[...])
    heads = q_ref.shape[-1] // Q_SLOT
    qs = (MLA_NOPE + MLA_ROPE) ** -0.5 * LOG2E
    rc, rs = rc_ref[...] * qs, rs_ref[...] * qs
    sw0 = heads * Q_SLOT
    for hh in range(heads):
        lo = hh * Q_SLOT
        q_ref[:, lo:lo + MLA_NOPE] = (zq[:, lo:lo + MLA_NOPE] * qs).astype(BF16)
        main = zq[:, lo + MLA_NOPE:lo + Q_SLOT]
        swp = zq[:, sw0 + hh * V7X_LANES:sw0 + (hh + 1) * V7X_LANES]
        q_ref[:, lo + MLA_NOPE:lo + Q_SLOT] = (main * rc + swp * rs).astype(BF16)


def _b_in(x2, rc, rs, norms, w_in, g_q, w_q, layer, j):
    t, d = x2.shape
    n = w_in.shape[-1]
    rank = g_q.shape[-1]
    nq = w_q.shape[-1]
    heads = nq // (Q_SLOT + V7X_LANES)
    tm = _tile(t, MIX_TOKEN_TILE)
    row = lambda i: (i, 0)
    return pl.pallas_call(
        _b_in_kernel,
        grid=(t // tm,),
        in_specs=[
            pl.BlockSpec((tm, d), row),
            pl.BlockSpec((tm, V7X_LANES), row),
            pl.BlockSpec((tm, V7X_LANES), row),
            _const_spec((None, None, 1, d), lambda i: (layer, 2, 0, 0)),
            _const_spec((None, d, n), lambda i: (j, 0, 0)),
            _const_spec((None, 1, rank), lambda i: (j, 0, 0)),
            _const_spec((None, rank, nq), lambda i: (j, 0, 0)),
        ],
        out_specs=[
            pl.BlockSpec((tm, heads * Q_SLOT), row),
            pl.BlockSpec((tm, MEM_WIDTH), row),
        ],
        out_shape=[
            jax.ShapeDtypeStruct((t, heads * Q_SLOT), BF16),
            jax.ShapeDtypeStruct((t, MEM_WIDTH), BF16),
        ],
        compiler_params=_cparams(("parallel",)),
        name="b_in",
    )(x2, rc, rs, norms, w_in, g_q, w_q)


ATTN_BLOCK = 512


def _attn_kernel(q_ref, k_ref, v_ref, o_ref, vx_ref, s_ref, *, blk):
    kv = 2 * blk
    n_qblocks = q_ref.shape[0] // kv

    vx_ref[:, :MLA_V] = v_ref[...]
    lane = lax.broadcasted_iota(jnp.int32, (vx_ref.shape[0], vx_ref.shape[1] - MLA_V), 1)
    vx_ref[:, MLA_V:] = jnp.where(lane == 0, 1.0, 0.0).astype(BF16)

    def keys_of(j, n):
        return pl.ds(j * kv, n)

    def scores(i, chain, j):
        n_keys = blk if (chain == 0 and j == i) else kv
        q_rows = pl.ds(i * kv + chain * blk, blk)
        s_ref[chain, :, :n_keys] = _dot_nt(q_ref[q_rows, :], k_ref[keys_of(j, n_keys), :])

    def absorb(chain, j, carry, n_keys, diag_offset=None):
        m, acc = carry
        s = s_ref[chain, :, :n_keys]
        if diag_offset is not None:
            rows = lax.broadcasted_iota(jnp.int32, s.shape, 0)
            cols = lax.broadcasted_iota(jnp.int32, s.shape, 1)
            s = jnp.where(cols <= rows + diag_offset, s, -jnp.inf)
        m_new = jnp.maximum(m, jnp.max(s, axis=-1, keepdims=True))
        p = jnp.exp2(s - m_new).astype(BF16)
        acc = jnp.exp2(m - m_new) * acc + _dot(p, vx_ref[keys_of(j, n_keys), :])
        return m_new, acc

    init = (jnp.full((blk, 1), -jnp.inf, F32), jnp.zeros((blk, vx_ref.shape[1]), F32))
    scores(0, 0, 0)
    for i in range(n_qblocks):
        ca, cb = init, init
        for j in range(i):
            scores(i, 1, j)
            ca = absorb(0, j, ca, kv)
            scores(i, 0, j + 1)
            cb = absorb(1, j, cb, kv)
        scores(i, 1, i)
        ca = absorb(0, i, ca, blk, diag_offset=0)
        if i + 1 < n_qblocks:
            scores(i + 1, 0, 0)
        cb = absorb(1, i, cb, kv, diag_offset=blk)
        for chain, (_, acc) in enumerate((ca, cb)):
            out_rows = pl.ds(i * kv + chain * blk, blk)
            o_ref[out_rows, :] = (acc[:, :MLA_V] / acc[:, MLA_V:MLA_V + 1]).astype(BF16)


def _attn(q, k_all, v, batch, seq):
    t = q.shape[0]
    heads = v.shape[-1] // MLA_V
    blk = _tile(seq // 2, ATTN_BLOCK)
    per_head = lambda b, h: (b, h)
    return pl.pallas_call(
        functools.partial(_attn_kernel, blk=blk),
        grid=(batch, heads),
        in_specs=[
            pl.BlockSpec((seq, Q_SLOT), per_head),
            pl.BlockSpec((seq, Q_SLOT), per_head),
            pl.BlockSpec((seq, MLA_V), per_head),
        ],
        out_specs=pl.BlockSpec((seq, MLA_V), per_head),
        out_shape=jax.ShapeDtypeStruct((t, heads * MLA_V), BF16),
        scratch_shapes=[pltpu.VMEM((seq, 2 * MLA_V), BF16), pltpu.VMEM((2, blk, 2 * blk), F32)],
        compiler_params=_cparams(("parallel", "parallel")),
        name="attn",
    )(q, k_all, v)


def _pad_cols(w, n):
    return jnp.pad(w, ((0, 0),) * (w.ndim - 1) + ((0, n - w.shape[-1]),))


def _q_weight(w_uq):
    nb, r, n = w_uq.shape
    heads = n // (MLA_NOPE + MLA_ROPE)
    w = w_uq.reshape(nb, r, heads, MLA_NOPE + MLA_ROPE)
    half = MLA_ROPE // 2
    nope, t1, t2 = w[..., :MLA_NOPE], w[..., MLA_NOPE:MLA_NOPE + half], w[..., MLA_NOPE + half:]
    main = _pad_cols(jnp.concatenate([nope, t1, t2], axis=-1), Q_SLOT).reshape(nb, r, heads * Q_SLOT)
    swp = _pad_cols(jnp.concatenate([t2, t1], axis=-1), V7X_LANES).reshape(nb, r, heads * V7X_LANES)
    return jnp.concatenate([main, swp], axis=-1)


def kernel(x, mem, positions, norms, ffn_w_gate, ffn_w_up, ffn_w_down, w_out, mem_norm, mem_w_kv, a_w_in, s5_lambda_re, s5_lambda_im, s5_b_re, s5_b_im, s5_c_re, s5_c_im, s5_d, s5_log_dt, s5_w_glu, s5_b_glu, b_w_in, mla_q_norm, mla_w_uq, kv_in_norm, w_dkv, kv_norm, w_uk, w_uv, w_kr):
    bsz, seq, d = x.shape
    t = bsz * seq
    depth = norms.shape[0]
    n_a = a_w_in.shape[0]
    n_b = b_w_in.shape[0]
    L = S5_CHUNK
    nj = seq // L
    half = MLA_ROPE // 2

    norms4 = norms.reshape(depth, norms.shape[1], 1, d)
    w_out_b = w_out.astype(BF16)

    ffn_state = {"w": (ffn_w_gate[0, 0].astype(BF16), ffn_w_up[0, 0].astype(BF16),
                       ffn_w_down[0, 0].astype(BF16))}

    def ffn(xx, layer, hf):
        nxt = (layer, 1) if hf == 0 else (layer + 1, 0)
        next_f32 = (ffn_w_gate, ffn_w_up, ffn_w_down) + nxt if nxt[0] < depth else None
        xx, ffn_state["w"] = _ffn(xx, norms4, ffn_state["w"], layer, hf, next_f32)
        return xx

    mkv_all = _mem_kv(mem.reshape(bsz * mem.shape[1], d), mem_norm.reshape(depth, 1, d),
                      mem_w_kv.astype(BF16))

    xp = x.reshape(bsz, nj, L, d).transpose(0, 2, 1, 3).reshape(t, d)
    a_w_in_b = a_w_in.astype(BF16)
    w_glu_b = s5_w_glu.astype(BF16)
    b_glu3 = s5_b_glu.reshape(n_a, 1, -1)
    for l in range(n_a):
        prep = _s5_prep(s5_lambda_re[l], s5_lambda_im[l], s5_b_re[l], s5_b_im[l], s5_c_re[l], s5_c_im[l],
                        s5_d[l], s5_log_dt[l], nj)
        xp = ffn(xp, l, 0)
        u, qm = _a_in(xp, norms4, a_w_in_b, l)
        y = _s5(u, prep, bsz)
        xp = _mix_out(xp, y, qm, mkv_all, norms4, w_out_b, l, seq, glu=(w_glu_b, b_glu3, l))
        xp = ffn(xp, l, 1)
    x2 = xp.reshape(bsz, L, nj, d).transpose(0, 2, 1, 3).reshape(t, d)

    inv_freq = ROPE_THETA ** (-jnp.arange(0, MLA_ROPE, 2, dtype=F32) / MLA_ROPE)
    zeros = jnp.zeros((V7X_LANES - MLA_ROPE,), F32)
    frq = jnp.concatenate([inv_freq, inv_freq, zeros]).reshape(1, V7X_LANES)
    sgn = jnp.concatenate([-jnp.ones((half,), F32), jnp.ones((half,), F32), zeros]).reshape(1, V7X_LANES)
    kr1, kr2 = w_kr[:, :half], w_kr[:, half:]
    w_kr2 = jnp.concatenate([_pad_cols(w_kr, V7X_LANES),
                             _pad_cols(jnp.concatenate([kr2, kr1], axis=-1), V7X_LANES)], axis=-1).astype(BF16)
    k_all, v, rc, rs = _kv(x2, positions.reshape(t, 1), frq, sgn, kv_in_norm.reshape(1, d),
                           w_dkv.astype(BF16), kv_norm.reshape(1, -1), w_uk.astype(BF16),
                           w_uv.astype(BF16), w_kr2)

    b_w_in_b = b_w_in.astype(BF16)
    w_q = _q_weight(mla_w_uq).astype(BF16)
    g_q = mla_q_norm.reshape(n_b, 1, -1)
    for j in range(n_b):
        l = n_a + j
        x2 = ffn(x2, l, 0)
        q, qm = _b_in(x2, rc, rs, norms4, b_w_in_b, g_q, w_q, l, j)
        o = _attn(q, k_all, v, bsz, seq)
        x2 = _mix_out(x2, o, qm, mkv_all, norms4, w_out_b, l, seq)
        x2 = ffn(x2, l, 1)
    return x2.reshape(bsz, seq, d)
```
